```python
import jax, jax.numpy as jnp
from jax import lax
import numpy as np

D_MODEL = 1024
BATCH = 4
SEQ = 8192
DEPTH = 2
DEC_BATCH = 32
DEC_SEQ = 16
PAST_LEN = 2048

CHUNK = 64
WINDOW = 128
WINDOW_CHUNKS = WINDOW // CHUNK
HEAD_DIM = 64
ATTN_WIDTH = D_MODEL // 2
N_HEADS = ATTN_WIDTH // HEAD_DIM
N_KV_HEADS = max(1, N_HEADS // 4)
GROUP = N_HEADS // N_KV_HEADS
CONV_CH = D_MODEL - ATTN_WIDTH
CONV_WIDTH = 31
CONV_STATE = CONV_WIDTH - 1
ROT_DIM = HEAD_DIM // 4
ROPE_THETA = 500000.0
D_FF = 4 * D_MODEL
EPS = 1e-6
Q_COLS = N_HEADS * HEAD_DIM
KV_COLS = N_KV_HEADS * HEAD_DIM
IN_COLS = Q_COLS + 2 * KV_COLS + 2 * CONV_CH

kernel_name = 'hymba_swa_sink_conformer_conv_stream_step'


def rms_norm(x, g):
    x32 = x.astype(jnp.float32)
    y = x32 * lax.rsqrt(jnp.mean(x32 * x32, axis=-1, keepdims=True) + EPS)
    return (y * g.astype(jnp.float32)).astype(x.dtype)


def partial_rope(x, pos):
    half = ROT_DIM // 2
    inv = jnp.power(jnp.float32(ROPE_THETA), -jnp.arange(half, dtype=jnp.float32) * 2.0 / ROT_DIM)
    ang = pos.astype(jnp.float32)[:, None] * inv[None, :]
    cos = jnp.cos(ang)[:, None, :]
    sin = jnp.sin(ang)[:, None, :]
    x32 = x.astype(jnp.float32)
    x1 = x32[..., :half]
    x2 = x32[..., half:ROT_DIM]
    out = jnp.concatenate([x1 * cos - x2 * sin, x2 * cos + x1 * sin, x32[..., ROT_DIM:]], axis=-1)
    return out.astype(x.dtype)


def project(h, w_in, q_g, k_g, pos):
    B, T, _ = h.shape
    z = h @ w_in
    q, k, v, a, b = jnp.split(z, [Q_COLS, Q_COLS + KV_COLS, Q_COLS + 2 * KV_COLS,
                                  Q_COLS + 2 * KV_COLS + CONV_CH], axis=-1)
    q = partial_rope(rms_norm(q.reshape(B, T, N_HEADS, HEAD_DIM), q_g), pos)
    k = partial_rope(rms_norm(k.reshape(B, T, N_KV_HEADS, HEAD_DIM), k_g), pos)
    v = v.reshape(B, T, N_KV_HEADS, HEAD_DIM)
    u = a * jax.nn.sigmoid(b)
    return q, k, v, u


def sink_attention(q, k, v, sinks, valid):
    s = jnp.einsum('...qkgd,...jkd->...kgqj', q.astype(jnp.float32), k.astype(jnp.float32)) * (HEAD_DIM ** -0.5)
    if valid is not None:
        s = jnp.where(valid, s, -jnp.inf)
    sink = sinks.astype(jnp.float32).reshape(N_KV_HEADS, GROUP, 1, 1)
    m = jnp.maximum(jnp.max(s, axis=-1, keepdims=True), sink)
    p = jnp.exp(s - m)
    denom = jnp.sum(p, axis=-1, keepdims=True) + jnp.exp(sink - m)
    o = jnp.einsum('...kgqj,...jkd->...qkgd', p / denom, v.astype(jnp.float32))
    return o.astype(q.dtype)


def band_attention_prompt(q, k, v, sinks):
    B, S = q.shape[0], q.shape[1]
    nc = S // CHUNK
    qc = q.reshape(B, nc, CHUNK, N_KV_HEADS, GROUP, HEAD_DIM)
    pad = ((0, 0), (WINDOW, 0), (0, 0), (0, 0))
    kc = jnp.pad(k, pad).reshape(B, nc + WINDOW_CHUNKS, CHUNK, N_KV_HEADS, HEAD_DIM)
    vc = jnp.pad(v, pad).reshape(B, nc + WINDOW_CHUNKS, CHUNK, N_KV_HEADS, HEAD_DIM)
    kb = jnp.concatenate([kc[:, j:j + nc] for j in range(WINDOW_CHUNKS + 1)], axis=2)
    vb = jnp.concatenate([vc[:, j:j + nc] for j in range(WINDOW_CHUNKS + 1)], axis=2)
    key_pos = jnp.arange(nc)[:, None] * CHUNK - WINDOW + jnp.arange(WINDOW + CHUNK)[None, :]
    valid = (key_pos >= 0)[None, :, None, None, None, :]
    o = sink_attention(qc, kb, vb, sinks, valid)
    return o.reshape(B, S, ATTN_WIDTH)


def attention_sample(q, k_all, v_all, sinks):
    B, T = q.shape[0], q.shape[1]
    qg = q.reshape(B, T, N_KV_HEADS, GROUP, HEAD_DIM)
    o = sink_attention(qg, k_all, v_all, sinks, None)
    return o.reshape(B, T, ATTN_WIDTH)


def conv_tail(u_ext, w, b, ln_g, ln_b):
    y = lax.conv_general_dilated(u_ext, w[:, None, :], (1,), 'VALID',
                                 dimension_numbers=('NWC', 'WIO', 'NWC'),
                                 feature_group_count=CONV_CH) + b
    y32 = y.astype(jnp.float32)
    mu = jnp.mean(y32, axis=-1, keepdims=True)
    var = jnp.mean(jnp.square(y32 - mu), axis=-1, keepdims=True)
    yn = (y32 - mu) * lax.rsqrt(var + EPS) * ln_g.astype(jnp.float32) + ln_b.astype(jnp.float32)
    return jax.nn.silu(yn).astype(u_ext.dtype)


def finish_layer(x, attn_o, conv_o, beta_a, beta_c, w_out, g2, w_up, w_down):
    x = x + jnp.concatenate([attn_o * beta_a, conv_o * beta_c], axis=-1) @ w_out
    h = rms_norm(x, g2)
    return x + jnp.square(jax.nn.relu(h @ w_up)) @ w_down


def setup_inputs(seed: int = 0) -> dict:
    key = jax.random.key(seed)
    ks = jax.random.split(key, 24)
    f32 = jnp.float32
    nrm = lambda k, shape, s: jax.random.normal(k, shape, f32) * s
    return {
        'x_prompt': nrm(ks[0], (BATCH, SEQ, D_MODEL), 1.0),
        'x_sample': nrm(ks[1], (DEC_BATCH, DEC_SEQ, D_MODEL), 1.0),
        'cache_k': nrm(ks[2], (DEPTH, DEC_BATCH, WINDOW, N_KV_HEADS, HEAD_DIM), 1.0),
        'cache_v': nrm(ks[3], (DEPTH, DEC_BATCH, WINDOW, N_KV_HEADS, HEAD_DIM), 1.0),
        'state_conv': nrm(ks[4], (DEPTH, DEC_BATCH, CONV_STATE, CONV_CH), 0.5),
        'norm1_g': 1.0 + nrm(ks[5], (DEPTH, D_MODEL), 0.02),
        'w_in': nrm(ks[6], (DEPTH, D_MODEL, IN_COLS), D_MODEL ** -0.5),
        'q_norm_g': 1.0 + nrm(ks[7], (DEPTH, HEAD_DIM), 0.02),
        'k_norm_g': 1.0 + nrm(ks[8], (DEPTH, HEAD_DIM), 0.02),
        'attn_sinks': nrm(ks[9], (DEPTH, N_HEADS), 1.0),
        'conv_w': nrm(ks[10], (DEPTH, CONV_WIDTH, CONV_CH), CONV_WIDTH ** -0.5),
        'conv_b': nrm(ks[11], (DEPTH, CONV_CH), 0.02),
        'conv_ln_g': 1.0 + nrm(ks[12], (DEPTH, CONV_CH), 0.02),
        'conv_ln_b': nrm(ks[13], (DEPTH, CONV_CH), 0.02),
        'beta_attn': 1.0 + nrm(ks[14], (DEPTH, ATTN_WIDTH), 0.02),
        'beta_conv': 1.0 + nrm(ks[15], (DEPTH, CONV_CH), 0.02),
        'w_out': nrm(ks[16], (DEPTH, D_MODEL, D_MODEL), D_MODEL ** -0.5),
        'norm2_g': 1.0 + nrm(ks[17], (DEPTH, D_MODEL), 0.02),
        'w_up': nrm(ks[18], (DEPTH, D_MODEL, D_FF), D_MODEL ** -0.5),
        'w_down': nrm(ks[19], (DEPTH, D_FF, D_MODEL), D_FF ** -0.5),
    }


def reference(x_prompt, x_sample, cache_k, cache_v, state_conv, norm1_g, w_in, q_norm_g, k_norm_g,
              attn_sinks, conv_w, conv_b, conv_ln_g, conv_ln_b, beta_attn, beta_conv, w_out,
              norm2_g, w_up, w_down):
    S = x_prompt.shape[1]
    T = x_sample.shape[1]
    pos_p = jnp.arange(S)
    pos_s = PAST_LEN + jnp.arange(T)
    yp, ys = x_prompt, x_sample
    pk, pv, pc, sk, sv, sc = [], [], [], [], [], []
    for l in range(DEPTH):
        h = rms_norm(yp, norm1_g[l])
        q, k, v, u = project(h, w_in[l], q_norm_g[l], k_norm_g[l], pos_p)
        a_o = band_attention_prompt(q, k, v, attn_sinks[l])
        c_o = conv_tail(jnp.pad(u, ((0, 0), (CONV_STATE, 0), (0, 0))), conv_w[l], conv_b[l], conv_ln_g[l], conv_ln_b[l])
        yp = finish_layer(yp, a_o, c_o, beta_attn[l], beta_conv[l], w_out[l], norm2_g[l], w_up[l], w_down[l])
        pk.append(k[:, -WINDOW:])
        pv.append(v[:, -WINDOW:])
        pc.append(u[:, -CONV_STATE:])
        h = rms_norm(ys, norm1_g[l])
        q, k, v, u = project(h, w_in[l], q_norm_g[l], k_norm_g[l], pos_s)
        k_all = jnp.concatenate([cache_k[l].astype(k.dtype), k], axis=1)
        v_all = jnp.concatenate([cache_v[l].astype(v.dtype), v], axis=1)
        a_o = attention_sample(q, k_all, v_all, attn_sinks[l])
        u_ext = jnp.concatenate([state_conv[l].astype(u.dtype), u], axis=1)
        c_o = conv_tail(u_ext, conv_w[l], conv_b[l], conv_ln_g[l], conv_ln_b[l])
        ys = finish_layer(ys, a_o, c_o, beta_attn[l], beta_conv[l], w_out[l], norm2_g[l], w_up[l], w_down[l])
        sk.append(k_all[:, -WINDOW:])
        sv.append(v_all[:, -WINDOW:])
        sc.append(u_ext[:, -CONV_STATE:])
    return (yp, ys, jnp.stack(pk), jnp.stack(pv), jnp.stack(pc), jnp.stack(sk), jnp.stack(sv), jnp.stack(sc))
```

```python
import functools

import jax
import jax.numpy as jnp
from jax import lax
from jax.experimental import pallas as pl
from jax.experimental.pallas import tpu as pltpu

D_MODEL = 1024
DEPTH = 2
PAST_LEN = 2048
CHUNK = 64
WINDOW = 128
HEAD_DIM = 64
ATTN_WIDTH = D_MODEL // 2
N_HEADS = ATTN_WIDTH // HEAD_DIM
N_KV_HEADS = max(1, N_HEADS // 4)
GROUP = N_HEADS // N_KV_HEADS
CONV_CH = D_MODEL - ATTN_WIDTH
CONV_WIDTH = 31
CONV_STATE = CONV_WIDTH - 1
ROT_DIM = HEAD_DIM // 4
ROPE_THETA = 500000.0
D_FF = 4 * D_MODEL
EPS = 1e-6
Q_COLS = N_HEADS * HEAD_DIM
KV_COLS = N_KV_HEADS * HEAD_DIM
IN_COLS = Q_COLS + 2 * KV_COLS + 2 * CONV_CH

LANES = 128
CONV_HALO = 32
CONV_SHIFT = CONV_HALO - CONV_STATE
PROMPT_TILE = 256
CONV_ROWS = 64
VMEM_LIMIT_BYTES = 56 * 1024 * 1024

F32 = jnp.float32
BF16 = jnp.bfloat16


def _dot(a, b):
    return jnp.dot(a, b, preferred_element_type=F32)


def _dot_nt(a, b):
    return lax.dot_general(a, b, (((1,), (1,)), ((), ())), preferred_element_type=F32)


def _rms_norm(x, g):
    ms = jnp.mean(x * x, axis=-1, keepdims=True)
    return x * lax.rsqrt(ms + EPS) * g


def _lane_lo(shape):
    return lax.broadcasted_iota(jnp.int32, shape, len(shape) - 1) % LANES < HEAD_DIM


def _head_norm_rope(z, gain, seg_ones, cos, sin_prev, sin_next):
    out = []
    for i in range(z.shape[1] // LANES):
        zi = z[:, i * LANES:(i + 1) * LANES]
        ssq = _dot((zi * zi).astype(BF16), seg_ones)
        zn = zi * lax.rsqrt(ssq * (1.0 / HEAD_DIM) + EPS) * gain
        prev = pltpu.roll(zn, ROT_DIM // 2, axis=1)
        nxt = pltpu.roll(zn, LANES - ROT_DIM // 2, axis=1)
        out.append(zn * cos + prev * sin_prev + nxt * sin_next)
    return out[0] if len(out) == 1 else jnp.concatenate(out, axis=1)


def _dup_heads(kv):
    swapped = pltpu.roll(kv, HEAD_DIM, axis=1)
    lo = _lane_lo(kv.shape)
    return jnp.where(lo, kv, swapped), jnp.where(lo, swapped, kv)


def _sink_column(sinks_ref, g, rows_per_head):
    ridx = lax.broadcasted_iota(jnp.int32, (GROUP * rows_per_head, 1), 0)
    col = jnp.full((GROUP * rows_per_head, 1), sinks_ref[g * GROUP + GROUP - 1], F32)
    for h in range(GROUP - 2, -1, -1):
        col = jnp.where(ridx < (h + 1) * rows_per_head, sinks_ref[g * GROUP + h], col)
    return col


def _group_attention(qa, qb, kd, vd, sink_col, bias):
    rows = qa.shape[0]
    lo = _lane_lo(qa.shape)
    zero = jnp.zeros_like(qa)
    qs = jnp.concatenate([jnp.where(lo, qa, zero), jnp.where(lo, zero, qa),
                          jnp.where(lo, qb, zero), jnp.where(lo, zero, qb)], axis=0)
    s = _dot_nt(qs, kd)
    if bias is not None:
        s = s + bias
    m = jnp.maximum(jnp.max(s, axis=-1, keepdims=True), sink_col)
    p = jnp.exp(s - m)
    denom = jnp.sum(p, axis=-1, keepdims=True) + jnp.exp(sink_col - m)
    o = _dot(p.astype(BF16), vd) * (1.0 / denom)
    oa = jnp.where(lo, o[0:rows], o[rows:2 * rows])
    ob = jnp.where(lo, o[2 * rows:3 * rows], o[3 * rows:4 * rows])
    return oa, ob


def _conv_ln_silu(u_window, cw_ref, cb, ln_g, ln_b):
    acc = u_window(0) * cw_ref[0:1, :]
    for j in range(1, CONV_WIDTH):
        acc = acc + u_window(j) * cw_ref[j:j + 1, :]
    y = acc + cb
    mu = jnp.mean(y, axis=-1, keepdims=True)
    yc = y - mu
    var = jnp.mean(yc * yc, axis=-1, keepdims=True)
    yn = yc * lax.rsqrt(var + EPS) * ln_g + ln_b
    return yn * jax.nn.sigmoid(yn)


def _ffn(x1, g2, wup_ref, wdn_ref):
    h2 = _rms_norm(x1, g2).astype(BF16)
    up = jnp.maximum(_dot(h2, wup_ref[...]), 0.0)
    act = (up * up).astype(BF16)
    return x1 + _dot(act, wdn_ref[...])


def _prompt_layer_kernel(sinks_ref, x_ref, cos_ref, sp_ref, sn_ref, g1_ref, win_ref, qg_ref, kg_ref,
                         seg_ref, cw_ref, cb_ref, lng_ref, lnb_ref, beta_ref, wout_ref, g2_ref,
                         wup_ref, wdn_ref,
                         y_ref, pk_ref, pv_ref, pc_ref,
                         q_scr, kd_scr, vd_scr, u_scr, cat_scr, *, tile):
    s_idx = pl.program_id(1)
    n_s = pl.num_programs(1)

    @pl.when(s_idx == 0)
    def _():
        kd_scr[:, 0:WINDOW, :] = jnp.zeros((N_KV_HEADS, WINDOW, LANES), BF16)
        vd_scr[:, 0:WINDOW, :] = jnp.zeros((N_KV_HEADS, WINDOW, LANES), BF16)
        u_scr[0:CONV_HALO, :] = jnp.zeros((CONV_HALO, CONV_CH), F32)

    x = x_ref[0]
    h = _rms_norm(x, g1_ref[...]).astype(BF16)
    z = _dot(h, win_ref[...])

    cos, sp, sn = cos_ref[...], sp_ref[...], sn_ref[...]
    seg = seg_ref[...]
    q = _head_norm_rope(z[:, :Q_COLS], qg_ref[...], seg, cos, sp, sn) * (HEAD_DIM ** -0.5)
    q_scr[...] = q.astype(BF16)
    k = _head_norm_rope(z[:, Q_COLS:Q_COLS + KV_COLS], kg_ref[...], seg, cos, sp, sn)
    v = z[:, Q_COLS + KV_COLS:Q_COLS + 2 * KV_COLS]
    for g, (kd, vd) in enumerate(zip(_dup_heads(k), _dup_heads(v))):
        kd_scr[g, WINDOW:WINDOW + tile, :] = kd.astype(BF16)
        vd_scr[g, WINDOW:WINDOW + tile, :] = vd.astype(BF16)
    a_off = Q_COLS + 2 * KV_COLS
    u = z[:, a_off:a_off + CONV_CH] * jax.nn.sigmoid(z[:, a_off + CONV_CH:a_off + 2 * CONV_CH])
    u_scr[CONV_HALO:CONV_HALO + tile, :] = u

    @pl.when(s_idx == n_s - 1)
    def _():
        pk_ref[0] = k[tile - WINDOW:tile, :]
        pv_ref[0] = v[tile - WINDOW:tile, :]
        pc_ref[0] = u_scr[CONV_HALO + tile - CONV_STATE:CONV_HALO + tile, :]

    beta = beta_ref[...]
    n_keys = WINDOW + CHUNK
    key_j = lax.broadcasted_iota(jnp.int32, (1, n_keys), 1)

    def attn_chunk(c, carry):
        r0 = pl.multiple_of(c * CHUNK, CHUNK)
        key_pos = s_idx * tile + r0 - WINDOW + key_j
        bias = jnp.where(key_pos >= 0, 0.0, -jnp.inf).astype(F32)
        for g in range(N_KV_HEADS):
            c0 = g * GROUP * HEAD_DIM
            qa = q_scr[pl.ds(r0, CHUNK), c0:c0 + LANES]
            qb = q_scr[pl.ds(r0, CHUNK), c0 + LANES:c0 + 2 * LANES]
            kd = kd_scr[g, pl.ds(r0, n_keys), :]
            vd = vd_scr[g, pl.ds(r0, n_keys), :]
            oa, ob = _group_attention(qa, qb, kd, vd, _sink_column(sinks_ref, g, CHUNK), bias)
            cat_scr[pl.ds(r0, CHUNK), c0:c0 + LANES] = (oa * beta[:, c0:c0 + LANES]).astype(BF16)
            cat_scr[pl.ds(r0, CHUNK), c0 + LANES:c0 + 2 * LANES] = (
                ob * beta[:, c0 + LANES:c0 + 2 * LANES]).astype(BF16)
        return carry

    lax.fori_loop(0, tile // CHUNK, attn_chunk, 0)

    cb, ln_g, ln_b = cb_ref[...], lng_ref[...], lnb_ref[...]
    beta_c = beta[:, ATTN_WIDTH:]
    for r in range(tile // CONV_ROWS):
        base = r * CONV_ROWS
        window = lambda j, base=base: u_scr[base + CONV_SHIFT + j:base + CONV_SHIFT + j + CONV_ROWS, :]
        c_o = _conv_ln_silu(window, cw_ref, cb, ln_g, ln_b)
        cat_scr[base:base + CONV_ROWS, ATTN_WIDTH:] = (c_o * beta_c).astype(BF16)

    for g in range(N_KV_HEADS):
        kd_scr[g, 0:WINDOW, :] = kd_scr[g, tile:tile + WINDOW, :]
        vd_scr[g, 0:WINDOW, :] = vd_scr[g, tile:tile + WINDOW, :]
    u_scr[0:CONV_HALO, :] = u_scr[tile:tile + CONV_HALO, :]

    x1 = x + _dot(cat_scr[...], wout_ref[...])
    y_ref[0] = _ffn(x1, g2_ref[...], wup_ref, wdn_ref)


def _sample_layer_kernel(sinks_ref, x_ref, ck_ref, cv_ref, st_ref, cos_ref, sp_ref, sn_ref, g1_ref,
                         win_ref, qg_ref, kg_ref, seg_ref, cw_ref, cb_ref, lng_ref, lnb_ref,
                         beta_ref, wout_ref, g2_ref, wup_ref, wdn_ref,
                         y_ref, sk_ref, sv_ref, sc_ref,
                         q_scr, kx_scr, vx_scr, ux_scr, cat_scr, *, n_seq, t_new):
    n_tok = n_seq * t_new
    x = x_ref[...]
    h = _rms_norm(x, g1_ref[...]).astype(BF16)
    z = _dot(h, win_ref[...])

    cos, sp, sn = cos_ref[...], sp_ref[...], sn_ref[...]
    seg = seg_ref[...]
    q = _head_norm_rope(z[:, :Q_COLS], qg_ref[...], seg, cos, sp, sn) * (HEAD_DIM ** -0.5)
    q_scr[...] = q.astype(BF16)
    k = _head_norm_rope(z[:, Q_COLS:Q_COLS + KV_COLS], kg_ref[...], seg, cos, sp, sn)
    v = z[:, Q_COLS + KV_COLS:Q_COLS + 2 * KV_COLS]
    a_off = Q_COLS + 2 * KV_COLS
    u = z[:, a_off:a_off + CONV_CH] * jax.nn.sigmoid(z[:, a_off + CONV_CH:a_off + 2 * CONV_CH])

    kx_scr[:, 0:WINDOW, :] = ck_ref[...]
    vx_scr[:, 0:WINDOW, :] = cv_ref[...]
    kx_scr[:, WINDOW:WINDOW + t_new, :] = k.reshape(n_seq, t_new, LANES)
    vx_scr[:, WINDOW:WINDOW + t_new, :] = v.reshape(n_seq, t_new, LANES)
    ux_scr[:, CONV_SHIFT:CONV_HALO, :] = st_ref[...]
    ux_scr[:, CONV_HALO:CONV_HALO + t_new, :] = u.reshape(n_seq, t_new, CONV_CH)
    sk_ref[...] = kx_scr[:, t_new:t_new + WINDOW, :]
    sv_ref[...] = vx_scr[:, t_new:t_new + WINDOW, :]
    sc_ref[...] = ux_scr[:, CONV_HALO + t_new - CONV_STATE:CONV_HALO + t_new, :]

    beta = beta_ref[...]
    cb, ln_g, ln_b = cb_ref[...], lng_ref[...], lnb_ref[...]
    beta_c = beta[:, ATTN_WIDTH:]

    def per_seq(b, carry):
        r0 = pl.multiple_of(b * t_new, t_new)
        kds = _dup_heads(kx_scr[b])
        vds = _dup_heads(vx_scr[b])
        for g in range(N_KV_HEADS):
            c0 = g * GROUP * HEAD_DIM
            qa = q_scr[pl.ds(r0, t_new), c0:c0 + LANES]
            qb = q_scr[pl.ds(r0, t_new), c0 + LANES:c0 + 2 * LANES]
            oa, ob = _group_attention(qa, qb, kds[g].astype(BF16), vds[g].astype(BF16),
                                      _sink_column(sinks_ref, g, t_new), None)
            cat_scr[pl.ds(r0, t_new), c0:c0 + LANES] = (oa * beta[:, c0:c0 + LANES]).astype(BF16)
            cat_scr[pl.ds(r0, t_new), c0 + LANES:c0 + 2 * LANES] = (
                ob * beta[:, c0 + LANES:c0 + 2 * LANES]).astype(BF16)
        window = lambda j: ux_scr[b, CONV_SHIFT + j:CONV_SHIFT + j + t_new, :]
        c_o = _conv_ln_silu(window, cw_ref, cb, ln_g, ln_b)
        cat_scr[pl.ds(r0, t_new), ATTN_WIDTH:] = (c_o * beta_c).astype(BF16)
        return carry

    lax.fori_loop(0, n_seq, per_seq, 0)

    x1 = x + _dot(cat_scr[...], wout_ref[...])
    y_ref[...] = _ffn(x1, g2_ref[...], wup_ref, wdn_ref)


def _resident(shape):
    nd = len(shape)
    return pl.BlockSpec(shape, lambda *_: (0,) * nd, pipeline_mode=pl.Buffered(1))


def _smem():
    return pl.BlockSpec(memory_space=pltpu.SMEM)


def _layer_weight_specs():
    return [
        _resident((1, D_MODEL)),
        _resident((D_MODEL, IN_COLS)),
        _resident((1, LANES)),
        _resident((1, LANES)),
        _resident((LANES, LANES)),
        _resident((CONV_WIDTH, CONV_CH)),
        _resident((1, CONV_CH)),
        _resident((1, CONV_CH)),
        _resident((1, CONV_CH)),
        _resident((1, D_MODEL)),
        _resident((D_MODEL, D_MODEL)),
        _resident((1, D_MODEL)),
        _resident((D_MODEL, D_FF)),
        _resident((D_FF, D_MODEL)),
    ]


def _prompt_layer(x, sinks, tables, weights, tile):
    batch, seq, _ = x.shape
    n_s = seq // tile
    table_spec = pl.BlockSpec((tile, LANES), lambda b, s: (s, 0))
    per_batch = lambda rows, cols: pl.BlockSpec((1, rows, cols), lambda b, s: (b, 0, 0))
    return pl.pallas_call(
        functools.partial(_prompt_layer_kernel, tile=tile),
        grid=(batch, n_s),
        in_specs=[_smem(), pl.BlockSpec((1, tile, D_MODEL), lambda b, s: (b, s, 0)),
                  table_spec, table_spec, table_spec] + _layer_weight_specs(),
        out_specs=[pl.BlockSpec((1, tile, D_MODEL), lambda b, s: (b, s, 0)),
                   per_batch(WINDOW, KV_COLS), per_batch(WINDOW, KV_COLS),
                   per_batch(CONV_STATE, CONV_CH)],
        out_shape=[jax.ShapeDtypeStruct((batch, seq, D_MODEL), F32),
                   jax.ShapeDtypeStruct((batch, WINDOW, KV_COLS), F32),
                   jax.ShapeDtypeStruct((batch, WINDOW, KV_COLS), F32),
                   jax.ShapeDtypeStruct((batch, CONV_STATE, CONV_CH), F32)],
        scratch_shapes=[pltpu.VMEM((tile, Q_COLS), BF16),
                        pltpu.VMEM((N_KV_HEADS, WINDOW + tile, LANES), BF16),
                        pltpu.VMEM((N_KV_HEADS, WINDOW + tile, LANES), BF16),
                        pltpu.VMEM((CONV_HALO + tile, CONV_CH), F32),
                        pltpu.VMEM((tile, D_MODEL), BF16)],
        compiler_params=pltpu.CompilerParams(
            dimension_semantics=("arbitrary", "arbitrary"),
            vmem_limit_bytes=VMEM_LIMIT_BYTES),
        name="prompt_layer",
    )(sinks, x, *tables, *weights)


def _sample_layer(x, cache_k, cache_v, state, sinks, tables, weights):
    n_seq, t_new, _ = x.shape
    n_tok = n_seq * t_new
    full = lambda *shape: pl.BlockSpec(shape, lambda i: (0,) * len(shape))
    y, sk, sv, sc = pl.pallas_call(
        functools.partial(_sample_layer_kernel, n_seq=n_seq, t_new=t_new),
        grid=(1,),
        in_specs=[_smem(), full(n_tok, D_MODEL), full(n_seq, WINDOW, KV_COLS),
                  full(n_seq, WINDOW, KV_COLS), full(n_seq, CONV_STATE, CONV_CH),
                  full(n_tok, LANES), full(n_tok, LANES), full(n_tok, LANES)] + _layer_weight_specs(),
        out_specs=[full(n_tok, D_MODEL), full(n_seq, WINDOW, KV_COLS), full(n_seq, WINDOW, KV_COLS),
                   full(n_seq, CONV_STATE, CONV_CH)],
        out_shape=[jax.ShapeDtypeStruct((n_tok, D_MODEL), F32),
                   jax.ShapeDtypeStruct((n_seq, WINDOW, KV_COLS), F32),
                   jax.ShapeDtypeStruct((n_seq, WINDOW, KV_COLS), F32),
                   jax.ShapeDtypeStruct((n_seq, CONV_STATE, CONV_CH), F32)],
        scratch_shapes=[pltpu.VMEM((n_tok, Q_COLS), BF16),
                        pltpu.VMEM((n_seq, WINDOW + t_new, LANES), F32),
                        pltpu.VMEM((n_seq, WINDOW + t_new, LANES), F32),
                        pltpu.VMEM((n_seq, CONV_HALO + t_new, CONV_CH), F32),
                        pltpu.VMEM((n_tok, D_MODEL), BF16)],
        compiler_params=pltpu.CompilerParams(
            dimension_semantics=("arbitrary",),
            vmem_limit_bytes=VMEM_LIMIT_BYTES),
        name="sample_layer",
    )(sinks, x.reshape(n_tok, D_MODEL), cache_k.reshape(n_seq, WINDOW, KV_COLS),
      cache_v.reshape(n_seq, WINDOW, KV_COLS), state, *tables, *weights)
    return y.reshape(n_seq, t_new, D_MODEL), sk, sv, sc


def _rope_tables(pos):
    half = ROT_DIM // 2
    inv = jnp.power(jnp.float32(ROPE_THETA), -jnp.arange(half, dtype=F32) * 2.0 / ROT_DIM)
    ang = pos.astype(F32)[:, None] * inv[None, :]
    cos, sin = jnp.cos(ang), jnp.sin(ang)
    n = pos.shape[0]
    pad = jnp.zeros((n, HEAD_DIM - ROT_DIM), F32)
    zeros = jnp.zeros((n, half), F32)
    cos_h = jnp.concatenate([cos, cos, pad + 1.0], axis=1)
    prev_h = jnp.concatenate([zeros, sin, pad], axis=1)
    next_h = jnp.concatenate([-sin, zeros, pad], axis=1)
    two = lambda t: jnp.concatenate([t, t], axis=1)
    return two(cos_h), two(prev_h), two(next_h)


def _layer_weights(l, norm1_g, w_in, q_norm_g, k_norm_g, conv_w, conv_b, conv_ln_g, conv_ln_b,
                   beta_attn, beta_conv, w_out, norm2_g, w_up, w_down):
    row = lambda a: a.reshape(1, -1).astype(F32)
    two_heads = lambda a: jnp.concatenate([a, a]).reshape(1, LANES).astype(F32)
    lane = jnp.arange(LANES) // HEAD_DIM
    seg_ones = (lane[:, None] == lane[None, :]).astype(BF16)
    return [row(norm1_g[l]), w_in[l].astype(BF16), two_heads(q_norm_g[l]), two_heads(k_norm_g[l]),
            seg_ones, conv_w[l].astype(F32), row(conv_b[l]), row(conv_ln_g[l]), row(conv_ln_b[l]),
            row(jnp.concatenate([beta_attn[l], beta_conv[l]])), w_out[l].astype(BF16),
            row(norm2_g[l]), w_up[l].astype(BF16), w_down[l].astype(BF16)]


def kernel(x_prompt, x_sample, cache_k, cache_v, state_conv, norm1_g, w_in, q_norm_g, k_norm_g,
           attn_sinks, conv_w, conv_b, conv_ln_g, conv_ln_b, beta_attn, beta_conv, w_out,
           norm2_g, w_up, w_down):
    batch, seq, _ = x_prompt.shape
    n_seq, t_new, _ = x_sample.shape
    tile = min(PROMPT_TILE, seq)
    assert seq % tile == 0 and tile % CHUNK == 0 and tile >= WINDOW and tile % CONV_ROWS == 0
    tables_p = _rope_tables(jnp.arange(seq))
    tables_s = tuple(jnp.tile(t, (n_seq, 1)) for t in _rope_tables(PAST_LEN + jnp.arange(t_new)))
    yp, ys = x_prompt, x_sample
    outs = [[] for _ in range(6)]
    for l in range(DEPTH):
        weights = _layer_weights(l, norm1_g, w_in, q_norm_g, k_norm_g, conv_w, conv_b, conv_ln_g,
                                 conv_ln_b, beta_attn, beta_conv, w_out, norm2_g, w_up, w_down)
        sinks = attn_sinks[l].astype(F32)
        yp, pk, pv, pc = _prompt_layer(yp, sinks, tables_p, weights, tile)
        ys, sk, sv, sc = _sample_layer(ys, cache_k[l], cache_v[l], state_conv[l], sinks, tables_s, weights)
        kv_shape = lambda a: a.reshape(a.shape[0], WINDOW, N_KV_HEADS, HEAD_DIM)
        for dst, val in zip(outs, (kv_shape(pk), kv_shape(pv), pc, kv_shape(sk), kv_shape(sv), sc)):
            dst.append(val)
    return (yp, ys) + tuple(jnp.stack(o) for o in outs)
```

```python
import functools

import jax
import jax.numpy as jnp
from jax import lax
from jax.experimental import pallas as pl
from jax.experimental.pallas import tpu as pltpu

D_MODEL = 1024
DEPTH = 2
PAST_LEN = 2048
CHUNK = 64
WINDOW = 128
HEAD_DIM = 64
ATTN_WIDTH = D_MODEL // 2
N_HEADS = ATTN_WIDTH // HEAD_DIM
N_KV_HEADS = max(1, N_HEADS // 4)
GROUP = N_HEADS // N_KV_HEADS
CONV_CH = D_MODEL - ATTN_WIDTH
CONV_WIDTH = 31
CONV_STATE = CONV_WIDTH - 1
ROT_DIM = HEAD_DIM // 4
ROPE_THETA = 500000.0
D_FF = 4 * D_MODEL
EPS = 1e-6
Q_COLS = N_HEADS * HEAD_DIM
KV_COLS = N_KV_HEADS * HEAD_DIM
IN_COLS = Q_COLS + 2 * KV_COLS + 2 * CONV_CH

LANES = 128
SUBLANES = 8
N_SLABS = Q_COLS // LANES
CONV_HALO = 32
CONV_SHIFT = CONV_HALO - CONV_STATE
PROMPT_TILE = 256
VMEM_LIMIT_BYTES = 56 * 1024 * 1024

F32 = jnp.float32
BF16 = jnp.bfloat16

assert N_KV_HEADS == 2 and KV_COLS == LANES and GROUP == N_SLABS


def _dot(a, b):
    return jnp.dot(a, b, preferred_element_type=F32)


def _dot_nt(a, b):
    return lax.dot_general(a, b, (((1,), (1,)), ((), ())), preferred_element_type=F32)


def _rms_norm(x, g):
    ms = jnp.mean(x * x, axis=-1, keepdims=True)
    return x * lax.rsqrt(ms + EPS) * g


def _lane_lo(shape):
    return lax.broadcasted_iota(jnp.int32, shape, len(shape) - 1) % LANES < HEAD_DIM


def _head_norm_rope(z, gain, seg_ones, cos, sin_prev, sin_next):
    out = []
    for i in range(z.shape[1] // LANES):
        zi = z[:, i * LANES:(i + 1) * LANES]
        ssq = _dot((zi * zi).astype(BF16), seg_ones)
        zn = zi * lax.rsqrt(ssq * (1.0 / HEAD_DIM) + EPS) * gain
        prev = pltpu.roll(zn, ROT_DIM // 2, axis=1)
        nxt = pltpu.roll(zn, LANES - ROT_DIM // 2, axis=1)
        out.append(zn * cos + prev * sin_prev + nxt * sin_next)
    return out[0] if len(out) == 1 else jnp.concatenate(out, axis=1)


def _sink_column(sinks_ref, rows):
    n_blocks = 2 * N_SLABS
    blk = lax.broadcasted_iota(jnp.int32, (n_blocks * rows, 1), 0) // rows
    col = jnp.zeros((n_blocks * rows, 1), F32)
    for i in range(N_SLABS):
        for half in range(2):
            col = jnp.where(blk == 2 * i + half, sinks_ref[i + GROUP * half], col)
    return col


def _attention(q, k, v, sink_col, valid):
    rows = q.shape[0]
    lo = _lane_lo((rows, LANES))
    zero = jnp.zeros((rows, LANES), q.dtype)
    parts = []
    for i in range(N_SLABS):
        slab = q[:, i * LANES:(i + 1) * LANES]
        parts += [jnp.where(lo, slab, zero), jnp.where(lo, zero, slab)]
    s = _dot_nt(jnp.concatenate(parts, axis=0), k)
    if valid is not None:
        s = jnp.where(valid, s, -jnp.inf)
    m = jnp.maximum(jnp.max(s, axis=-1, keepdims=True), sink_col)
    p = jnp.exp(s - m)
    denom = jnp.sum(p, axis=-1, keepdims=True) + jnp.exp(sink_col - m)
    o = _dot(p.astype(BF16), v) * (1.0 / denom)
    return jnp.concatenate(
        [jnp.where(lo, o[2 * i * rows:(2 * i + 1) * rows], o[(2 * i + 1) * rows:(2 * i + 2) * rows])
         for i in range(N_SLABS)], axis=1)


def _conv_ln_silu(u_window, cw_ref, cb, ln_g, ln_b):
    acc = u_window(0) * cw_ref[0:1, :]
    for j in range(1, CONV_WIDTH):
        acc = acc + u_window(j) * cw_ref[j:j + 1, :]
    y = acc + cb
    mu = jnp.mean(y, axis=-1, keepdims=True)
    yc = y - mu
    var = jnp.mean(yc * yc, axis=-1, keepdims=True)
    yn = yc * lax.rsqrt(var + EPS) * ln_g + ln_b
    return yn * jax.nn.sigmoid(yn)


def _ffn_part(h2, wup_ref, wdn_ref, c0, width):
    up = jnp.maximum(_dot(h2, wup_ref[:, c0:c0 + width]), 0.0)
    return _dot((up * up).astype(BF16), wdn_ref[c0:c0 + width, :])


def _prompt_layer_kernel(sinks_ref, x_ref, cos_ref, sp_ref, sn_ref, g1_ref, win_ref, qg_ref, kg_ref,
                         seg_ref, cw_ref, cb_ref, lng_ref, lnb_ref, beta_ref, wout_ref, g2_ref,
                         wup_ref, wdn_ref,
                         y_ref, pk_ref, pv_ref, pc_ref,
                         q_scr, k_scr, v_scr, u_scr, cat_scr, x1_scr, h2_scr, *, tile, n_s, n_tiles):
    t = pl.program_id(0)
    s_idx = jnp.minimum(t, n_tiles - 1) % n_s
    n_pieces = tile // CHUNK
    ffn_width = D_FF // n_pieces

    @pl.when(t == 0)
    def _():
        x1_scr[...] = jnp.zeros(x1_scr.shape, F32)
        h2_scr[...] = jnp.zeros(h2_scr.shape, BF16)

    @pl.when(s_idx == 0)
    def _():
        k_scr[0:WINDOW, :] = jnp.zeros((WINDOW, LANES), BF16)
        v_scr[0:WINDOW, :] = jnp.zeros((WINDOW, LANES), BF16)
        u_scr[0, 0:CONV_HALO, :] = jnp.zeros((CONV_HALO, CONV_CH), F32)

    h = _rms_norm(x_ref[0], g1_ref[...]).astype(BF16)
    z = _dot(h, win_ref[...])
    cos, sp, sn = cos_ref[...], sp_ref[...], sn_ref[...]
    seg = seg_ref[...]
    q = _head_norm_rope(z[:, :Q_COLS], qg_ref[...], seg, cos, sp, sn) * (HEAD_DIM ** -0.5)
    q_scr[...] = q.astype(BF16)
    k = _head_norm_rope(z[:, Q_COLS:Q_COLS + KV_COLS], kg_ref[...], seg, cos, sp, sn)
    v = z[:, Q_COLS + KV_COLS:Q_COLS + 2 * KV_COLS]
    k_scr[WINDOW:WINDOW + tile, :] = k.astype(BF16)
    v_scr[WINDOW:WINDOW + tile, :] = v.astype(BF16)
    pk_ref[0] = k[tile - WINDOW:tile, :]
    pv_ref[0] = v[tile - WINDOW:tile, :]
    a_off = Q_COLS + 2 * KV_COLS
    u = z[:, a_off:a_off + CONV_CH] * jax.nn.sigmoid(z[:, a_off + CONV_CH:a_off + 2 * CONV_CH])
    u_scr[0, CONV_HALO:CONV_HALO + tile, :] = u
    pc_ref[0] = u_scr[0, CONV_HALO + tile - CONV_STATE:CONV_HALO + tile, :]
    n_shift_rows = CONV_HALO + tile - SUBLANES
    for b in range(1, SUBLANES):
        u_scr[b, 0:n_shift_rows, :] = u_scr[0, b:b + n_shift_rows, :]

    beta = beta_ref[...]
    cb, ln_g, ln_b = cb_ref[...], lng_ref[...], lnb_ref[...]
    sink_col = _sink_column(sinks_ref, CHUNK)
    n_keys = WINDOW + CHUNK
    key_j = lax.broadcasted_iota(jnp.int32, (1, n_keys), 1)
    h2 = h2_scr[...]

    for c in range(n_pieces):
        r0 = c * CHUNK
        part = _ffn_part(h2, wup_ref, wdn_ref, c * ffn_width, ffn_width)
        if c == 0:
            y_ref[0] = x1_scr[...] + part
        else:
            y_ref[0] += part

        valid = None
        if r0 < WINDOW:
            valid = jnp.logical_or(key_j >= WINDOW - r0, s_idx > 0)
        a_o = _attention(q_scr[r0:r0 + CHUNK, :], k_scr[r0:r0 + n_keys, :], v_scr[r0:r0 + n_keys, :],
                         sink_col, valid)
        cat_scr[r0:r0 + CHUNK, 0:ATTN_WIDTH] = (a_o * beta[:, 0:ATTN_WIDTH]).astype(BF16)

        def window(j, r0=r0):
            off = CONV_SHIFT + j
            b = off % SUBLANES
            return u_scr[b, r0 + off - b:r0 + off - b + CHUNK, :]
        c_o = _conv_ln_silu(window, cw_ref, cb, ln_g, ln_b)
        cat_scr[r0:r0 + CHUNK, ATTN_WIDTH:] = (c_o * beta[:, ATTN_WIDTH:]).astype(BF16)

    k_scr[0:WINDOW, :] = k_scr[tile:tile + WINDOW, :]
    v_scr[0:WINDOW, :] = v_scr[tile:tile + WINDOW, :]
    u_scr[0, 0:CONV_HALO, :] = u_scr[0, tile:tile + CONV_HALO, :]
    x1 = x_ref[0] + _dot(cat_scr[...], wout_ref[...])
    x1_scr[...] = x1
    h2_scr[...] = _rms_norm(x1, g2_ref[...]).astype(BF16)


def _sample_layer_kernel(sinks_ref, x_ref, ck_ref, cv_ref, st_ref, cos_ref, sp_ref, sn_ref, g1_ref,
                         win_ref, qg_ref, kg_ref, seg_ref, cw_ref, cb_ref, lng_ref, lnb_ref,
                         beta_ref, wout_ref, g2_ref, wup_ref, wdn_ref,
                         y_ref, sk_ref, sv_ref, sc_ref,
                         q_scr, kx_scr, vx_scr, ux_scr, cat_scr, *, n_seq, t_new):
    x = x_ref[...]
    h = _rms_norm(x, g1_ref[...]).astype(BF16)
    z = _dot(h, win_ref[...])

    cos, sp, sn = cos_ref[...], sp_ref[...], sn_ref[...]
    seg = seg_ref[...]
    q = _head_norm_rope(z[:, :Q_COLS], qg_ref[...], seg, cos, sp, sn) * (HEAD_DIM ** -0.5)
    q_scr[...] = q.astype(BF16)
    k = _head_norm_rope(z[:, Q_COLS:Q_COLS + KV_COLS], kg_ref[...], seg, cos, sp, sn)
    v = z[:, Q_COLS + KV_COLS:Q_COLS + 2 * KV_COLS]
    a_off = Q_COLS + 2 * KV_COLS
    u = z[:, a_off:a_off + CONV_CH] * jax.nn.sigmoid(z[:, a_off + CONV_CH:a_off + 2 * CONV_CH])

    kx_scr[:, 0:WINDOW, :] = ck_ref[...]
    vx_scr[:, 0:WINDOW, :] = cv_ref[...]
    kx_scr[:, WINDOW:WINDOW + t_new, :] = k.reshape(n_seq, t_new, LANES)
    vx_scr[:, WINDOW:WINDOW + t_new, :] = v.reshape(n_seq, t_new, LANES)
    ux_scr[:, CONV_SHIFT:CONV_HALO, :] = st_ref[...]
    ux_scr[:, CONV_HALO:CONV_HALO + t_new, :] = u.reshape(n_seq, t_new, CONV_CH)
    sk_ref[...] = kx_scr[:, t_new:t_new + WINDOW, :]
    sv_ref[...] = vx_scr[:, t_new:t_new + WINDOW, :]
    sc_ref[...] = ux_scr[:, CONV_HALO + t_new - CONV_STATE:CONV_HALO + t_new, :]

    beta = beta_ref[...]
    cb, ln_g, ln_b = cb_ref[...], lng_ref[...], lnb_ref[...]
    sink_col = _sink_column(sinks_ref, t_new)

    def per_seq(b, carry):
        r0 = pl.multiple_of(b * t_new, t_new)
        a_o = _attention(q_scr[pl.ds(r0, t_new), :], kx_scr[b].astype(BF16), vx_scr[b].astype(BF16),
                         sink_col, None)
        cat_scr[pl.ds(r0, t_new), 0:ATTN_WIDTH] = (a_o * beta[:, 0:ATTN_WIDTH]).astype(BF16)
        window = lambda j: ux_scr[b, CONV_SHIFT + j:CONV_SHIFT + j + t_new, :]
        c_o = _conv_ln_silu(window, cw_ref, cb, ln_g, ln_b)
        cat_scr[pl.ds(r0, t_new), ATTN_WIDTH:] = (c_o * beta[:, ATTN_WIDTH:]).astype(BF16)
        return carry

    lax.fori_loop(0, n_seq, per_seq, 0)

    x1 = x + _dot(cat_scr[...], wout_ref[...])
    h2 = _rms_norm(x1, g2_ref[...]).astype(BF16)
    y_ref[...] = x1 + _ffn_part(h2, wup_ref, wdn_ref, 0, D_FF)


def _resident(shape):
    nd = len(shape)
    return pl.BlockSpec(shape, lambda *_: (0,) * nd, pipeline_mode=pl.Buffered(1))


def _smem():
    return pl.BlockSpec(memory_space=pltpu.SMEM)


def _layer_weight_specs():
    return [
        _resident((1, D_MODEL)),
        _resident((D_MODEL, IN_COLS)),
        _resident((1, LANES)),
        _resident((1, LANES)),
        _resident((LANES, LANES)),
        _resident((CONV_WIDTH, CONV_CH)),
        _resident((1, CONV_CH)),
        _resident((1, CONV_CH)),
        _resident((1, CONV_CH)),
        _resident((1, D_MODEL)),
        _resident((D_MODEL, D_MODEL)),
        _resident((1, D_MODEL)),
        _resident((D_MODEL, D_FF)),
        _resident((D_FF, D_MODEL)),
    ]


def _prompt_layer(x, sinks, tables, weights, tile):
    batch, seq, _ = x.shape
    n_s = seq // tile
    n_tiles = batch * n_s
    mixer_tile = lambda t: jnp.minimum(t, n_tiles - 1)
    ffn_tile = lambda t: jnp.maximum(t - 1, 0)
    table_spec = pl.BlockSpec((tile, LANES), lambda t: (mixer_tile(t) % n_s, 0))
    per_batch = lambda rows, cols: pl.BlockSpec((1, rows, cols), lambda t: (mixer_tile(t) // n_s, 0, 0))
    return pl.pallas_call(
        functools.partial(_prompt_layer_kernel, tile=tile, n_s=n_s, n_tiles=n_tiles),
        grid=(n_tiles + 1,),
        in_specs=[_smem(),
                  pl.BlockSpec((1, tile, D_MODEL), lambda t: (mixer_tile(t) // n_s, mixer_tile(t) % n_s, 0)),
                  table_spec, table_spec, table_spec] + _layer_weight_specs(),
        out_specs=[pl.BlockSpec((1, tile, D_MODEL), lambda t: (ffn_tile(t) // n_s, ffn_tile(t) % n_s, 0)),
                   per_batch(WINDOW, KV_COLS), per_batch(WINDOW, KV_COLS),
                   per_batch(CONV_STATE, CONV_CH)],
        out_shape=[jax.ShapeDtypeStruct((batch, seq, D_MODEL), F32),
                   jax.ShapeDtypeStruct((batch, WINDOW, KV_COLS), F32),
                   jax.ShapeDtypeStruct((batch, WINDOW, KV_COLS), F32),
                   jax.ShapeDtypeStruct((batch, CONV_STATE, CONV_CH), F32)],
        scratch_shapes=[pltpu.VMEM((tile, Q_COLS), BF16),
                        pltpu.VMEM((WINDOW + tile, LANES), BF16),
                        pltpu.VMEM((WINDOW + tile, LANES), BF16),
                        pltpu.VMEM((SUBLANES, CONV_HALO + tile, CONV_CH), F32),
                        pltpu.VMEM((tile, D_MODEL), BF16),
                        pltpu.VMEM((tile, D_MODEL), F32),
                        pltpu.VMEM((tile, D_MODEL), BF16)],
        compiler_params=pltpu.CompilerParams(
            dimension_semantics=("arbitrary",),
            vmem_limit_bytes=VMEM_LIMIT_BYTES),
        name="prompt_layer",
    )(sinks, x, *tables, *weights)


def _sample_layer(x, cache_k, cache_v, state, sinks, tables, weights):
    n_seq, t_new, _ = x.shape
    n_tok = n_seq * t_new
    full = lambda *shape: pl.BlockSpec(shape, lambda i: (0,) * len(shape))
    y, sk, sv, sc = pl.pallas_call(
        functools.partial(_sample_layer_kernel, n_seq=n_seq, t_new=t_new),
        grid=(1,),
        in_specs=[_smem(), full(n_tok, D_MODEL), full(n_seq, WINDOW, KV_COLS),
                  full(n_seq, WINDOW, KV_COLS), full(n_seq, CONV_STATE, CONV_CH),
                  full(n_tok, LANES), full(n_tok, LANES), full(n_tok, LANES)] + _layer_weight_specs(),
        out_specs=[full(n_tok, D_MODEL), full(n_seq, WINDOW, KV_COLS), full(n_seq, WINDOW, KV_COLS),
                   full(n_seq, CONV_STATE, CONV_CH)],
        out_shape=[jax.ShapeDtypeStruct((n_tok, D_MODEL), F32),
                   jax.ShapeDtypeStruct((n_seq, WINDOW, KV_COLS), F32),
                   jax.ShapeDtypeStruct((n_seq, WINDOW, KV_COLS), F32),
                   jax.ShapeDtypeStruct((n_seq, CONV_STATE, CONV_CH), F32)],
        scratch_shapes=[pltpu.VMEM((n_tok, Q_COLS), BF16),
                        pltpu.VMEM((n_seq, WINDOW + t_new, LANES), F32),
                        pltpu.VMEM((n_seq, WINDOW + t_new, LANES), F32),
                        pltpu.VMEM((n_seq, CONV_HALO + t_new, CONV_CH), F32),
                        pltpu.VMEM((n_tok, D_MODEL), BF16)],
        compiler_params=pltpu.CompilerParams(
            dimension_semantics=("arbitrary",),
            vmem_limit_bytes=VMEM_LIMIT_BYTES),
        name="sample_layer",
    )(sinks, x.reshape(n_tok, D_MODEL), cache_k.reshape(n_seq, WINDOW, KV_COLS),
      cache_v.reshape(n_seq, WINDOW, KV_COLS), state, *tables, *weights)
    return y.reshape(n_seq, t_new, D_MODEL), sk, sv, sc


def _rope_tables(pos):
    half = ROT_DIM // 2
    inv = jnp.power(jnp.float32(ROPE_THETA), -jnp.arange(half, dtype=F32) * 2.0 / ROT_DIM)
    ang = pos.astype(F32)[:, None] * inv[None, :]
    cos, sin = jnp.cos(ang), jnp.sin(ang)
    n = pos.shape[0]
    pad = jnp.zeros((n, HEAD_DIM - ROT_DIM), F32)
    zeros = jnp.zeros((n, half), F32)
    cos_h = jnp.concatenate([cos, cos, pad + 1.0], axis=1)
    prev_h = jnp.concatenate([zeros, sin, pad], axis=1)
    next_h = jnp.concatenate([-sin, zeros, pad], axis=1)
    two = lambda t: jnp.concatenate([t, t], axis=1)
    return two(cos_h), two(prev_h), two(next_h)


def _slab_order(a, axis):
    shape = a.shape
    heads = a.reshape(shape[:axis] + (N_KV_HEADS, GROUP, HEAD_DIM) + shape[axis + 1:])
    return jnp.swapaxes(heads, axis, axis + 1).reshape(shape)


def _layer_weights(l, norm1_g, w_in, q_norm_g, k_norm_g, conv_w, conv_b, conv_ln_g, conv_ln_b,
                   beta_attn, beta_conv, w_out, norm2_g, w_up, w_down):
    row = lambda a: a.reshape(1, -1).astype(F32)
    two_heads = lambda a: jnp.concatenate([a, a]).reshape(1, LANES).astype(F32)
    lane = jnp.arange(LANES) // HEAD_DIM
    seg_ones = (lane[:, None] == lane[None, :]).astype(BF16)
    w_in_l = jnp.concatenate([_slab_order(w_in[l][:, :Q_COLS], 1), w_in[l][:, Q_COLS:]], axis=1)
    w_out_l = jnp.concatenate([_slab_order(w_out[l][:ATTN_WIDTH], 0), w_out[l][ATTN_WIDTH:]], axis=0)
    beta = jnp.concatenate([_slab_order(beta_attn[l], 0), beta_conv[l]])
    return [row(norm1_g[l]), w_in_l.astype(BF16), two_heads(q_norm_g[l]), two_heads(k_norm_g[l]),
            seg_ones, conv_w[l].astype(F32), row(conv_b[l]), row(conv_ln_g[l]), row(conv_ln_b[l]),
            row(beta), w_out_l.astype(BF16), row(norm2_g[l]), w_up[l].astype(BF16), w_down[l].astype(BF16)]


def kernel(x_prompt, x_sample, cache_k, cache_v, state_conv, norm1_g, w_in, q_norm_g, k_norm_g,
           attn_sinks, conv_w, conv_b, conv_ln_g, conv_ln_b, beta_attn, beta_conv, w_out,
           norm2_g, w_up, w_down):
    batch, seq, _ = x_prompt.shape
    n_seq, t_new, _ = x_sample.shape
    tile = min(PROMPT_TILE, seq)
    assert seq % tile == 0 and tile % CHUNK == 0 and tile >= WINDOW and D_FF % (tile // CHUNK) == 0
    tables_p = _rope_tables(jnp.arange(seq))
    tables_s = tuple(jnp.tile(t, (n_seq, 1)) for t in _rope_tables(PAST_LEN + jnp.arange(t_new)))
    yp, ys = x_prompt, x_sample
    outs = [[] for _ in range(6)]
    for l in range(DEPTH):
        weights = _layer_weights(l, norm1_g, w_in, q_norm_g, k_norm_g, conv_w, conv_b, conv_ln_g,
                                 conv_ln_b, beta_attn, beta_conv, w_out, norm2_g, w_up, w_down)
        sinks = attn_sinks[l].astype(F32)
        yp, pk, pv, pc = _prompt_layer(yp, sinks, tables_p, weights, tile)
        ys, sk, sv, sc = _sample_layer(ys, cache_k[l], cache_v[l], state_conv[l], sinks, tables_s, weights)
        kv_shape = lambda a: a.reshape(a.shape[0], WINDOW, N_KV_HEADS, HEAD_DIM)
        for dst, val in zip(outs, (kv_shape(pk), kv_shape(pv), pc, kv_shape(sk), kv_shape(sv), sc)):
            dst.append(val)
    return (yp, ys) + tuple(jnp.stack(o) for o in outs)
```

```python
import functools

import jax
import jax.numpy as jnp
from jax import lax
from jax.experimental import pallas as pl
from jax.experimental.pallas import tpu as pltpu

D_MODEL = 1024
DEPTH = 2
PAST_LEN = 2048
CHUNK = 64
WINDOW = 128
HEAD_DIM = 64
ATTN_WIDTH = D_MODEL // 2
N_HEADS = ATTN_WIDTH // HEAD_DIM
N_KV_HEADS = max(1, N_HEADS // 4)
GROUP = N_HEADS // N_KV_HEADS
CONV_CH = D_MODEL - ATTN_WIDTH
CONV_WIDTH = 31
CONV_STATE = CONV_WIDTH - 1
ROT_DIM = HEAD_DIM // 4
ROPE_THETA = 500000.0
D_FF = 4 * D_MODEL
EPS = 1e-6
Q_COLS = N_HEADS * HEAD_DIM
KV_COLS = N_KV_HEADS * HEAD_DIM
IN_COLS = Q_COLS + 2 * KV_COLS + 2 * CONV_CH

LANES = 128
SUBLANES = 8
N_SLABS = Q_COLS // LANES
CONV_HALO = 32
CONV_SHIFT = CONV_HALO - CONV_STATE
PROMPT_TILE = 256
SAMPLE_UNROLL = 4
SAMPLE_FFN_PIECES = 4
VMEM_LIMIT_BYTES = 56 * 1024 * 1024

ROW_G1, ROW_G2, ROW_BETA, ROW_CONV_B, ROW_LN_G, ROW_LN_B, ROW_QG, ROW_KG = range(8)
N_VEC_ROWS = 8

F32 = jnp.float32
BF16 = jnp.bfloat16

assert N_KV_HEADS == 2 and KV_COLS == LANES and GROUP == N_SLABS


def _dot(a, b):
    return jnp.dot(a, b, preferred_element_type=F32)


def _dot_nt(a, b):
    return lax.dot_general(a, b, (((1,), (1,)), ((), ())), preferred_element_type=F32)


def _rms_norm(x, g):
    ms = jnp.mean(x * x, axis=-1, keepdims=True)
    return x * lax.rsqrt(ms + EPS) * g


def _lane_lo(shape):
    return lax.broadcasted_iota(jnp.int32, shape, len(shape) - 1) % LANES < HEAD_DIM


def _head_norm_rope(z, gain, seg_ones, cos, sin_prev, sin_next):
    out = []
    for i in range(z.shape[1] // LANES):
        zi = z[:, i * LANES:(i + 1) * LANES]
        ssq = _dot((zi * zi).astype(BF16), seg_ones)
        zn = zi * lax.rsqrt(ssq * (1.0 / HEAD_DIM) + EPS) * gain
        prev = pltpu.roll(zn, ROT_DIM // 2, axis=1)
        nxt = pltpu.roll(zn, LANES - ROT_DIM // 2, axis=1)
        out.append(zn * cos + prev * sin_prev + nxt * sin_next)
    return out[0] if len(out) == 1 else jnp.concatenate(out, axis=1)


def _sink_column(sinks_ref, layer, rows):
    n_blocks = 2 * N_SLABS
    blk = lax.broadcasted_iota(jnp.int32, (n_blocks * rows, 1), 0) // rows
    col = jnp.zeros((n_blocks * rows, 1), F32)
    for i in range(N_SLABS):
        for half in range(2):
            col = jnp.where(blk == 2 * i + half, sinks_ref[layer, i + GROUP * half], col)
    return col


def _attention(q, k, v, sink_col, valid):
    rows = q.shape[0]
    lo = _lane_lo((rows, LANES))
    zero = jnp.zeros((rows, LANES), q.dtype)
    parts = []
    for i in range(N_SLABS):
        slab = q[:, i * LANES:(i + 1) * LANES]
        parts += [jnp.where(lo, slab, zero), jnp.where(lo, zero, slab)]
    s = _dot_nt(jnp.concatenate(parts, axis=0), k)
    if valid is not None:
        s = jnp.where(valid, s, -jnp.inf)
    m = jnp.maximum(jnp.max(s, axis=-1, keepdims=True), sink_col)
    p = jnp.exp(s - m)
    denom = jnp.sum(p, axis=-1, keepdims=True) + jnp.exp(sink_col - m)
    o = _dot(p.astype(BF16), v) * (1.0 / denom)
    return jnp.concatenate(
        [jnp.where(lo, o[2 * i * rows:(2 * i + 1) * rows], o[(2 * i + 1) * rows:(2 * i + 2) * rows])
         for i in range(N_SLABS)], axis=1)


def _conv_ln_silu(u_window, cw_ref, cb, ln_g, ln_b):
    acc = u_window(0) * cw_ref[0:1, :]
    for j in range(1, CONV_WIDTH):
        acc = acc + u_window(j) * cw_ref[j:j + 1, :]
    y = acc + cb
    mu = jnp.mean(y, axis=-1, keepdims=True)
    yc = y - mu
    var = jnp.mean(yc * yc, axis=-1, keepdims=True)
    yn = yc * lax.rsqrt(var + EPS) * ln_g + ln_b
    return yn * jax.nn.sigmoid(yn)


def _ffn_part(h2, wup_ref, wdn_ref, c0, width):
    up = jnp.maximum(_dot(h2, wup_ref[:, c0:c0 + width]).astype(BF16), 0.0)
    return _dot(up * up, wdn_ref[c0:c0 + width, :])


def _vec(vec_ref, row, width=D_MODEL):
    return vec_ref[row:row + 1, 0:width]


def _prompt_layer_kernel(sinks_ref, x_ref, cos_ref, sp_ref, sn_ref, vec_ref, win_ref, seg_ref, cw_ref,
                         wout_ref, wup_ref, wdn_ref,
                         y_ref, pk_ref, pv_ref, pc_ref,
                         q_scr, k_scr, v_scr, u_scr, cat_scr, x1_scr, h2_scr,
                         *, layer, tile, n_s, n_tiles):
    t = pl.program_id(0)
    s_idx = jnp.minimum(t, n_tiles - 1) % n_s
    slot = t % 2
    n_pieces = tile // CHUNK
    ffn_width = D_FF // n_pieces

    @pl.when(t == 0)
    def _():
        x1_scr[...] = jnp.zeros(x1_scr.shape, F32)
        h2_scr[...] = jnp.zeros(h2_scr.shape, BF16)

    @pl.when(s_idx == 0)
    def _():
        k_scr[0:WINDOW, :] = jnp.zeros((WINDOW, LANES), BF16)
        v_scr[0:WINDOW, :] = jnp.zeros((WINDOW, LANES), BF16)
        u_scr[0, 0:CONV_HALO, :] = jnp.zeros((CONV_HALO, CONV_CH), F32)

    def ffn_piece(c):
        part = _ffn_part(h2_scr[slot], wup_ref, wdn_ref, c * ffn_width, ffn_width)
        if c == 0:
            y_ref[0] = x1_scr[...] + part
        else:
            y_ref[0] += part

    ffn_piece(0)

    h = _rms_norm(x_ref[0], _vec(vec_ref, ROW_G1)).astype(BF16)
    z = _dot(h, win_ref[...])
    cos, sp, sn = cos_ref[...], sp_ref[...], sn_ref[...]
    seg = seg_ref[...]
    q = _head_norm_rope(z[:, :Q_COLS], _vec(vec_ref, ROW_QG, LANES), seg, cos, sp, sn) * (HEAD_DIM ** -0.5)
    q_scr[...] = q.astype(BF16)
    k = _head_norm_rope(z[:, Q_COLS:Q_COLS + KV_COLS], _vec(vec_ref, ROW_KG, LANES), seg, cos, sp, sn)
    v = z[:, Q_COLS + KV_COLS:Q_COLS + 2 * KV_COLS]
    k_scr[WINDOW:WINDOW + tile, :] = k.astype(BF16)
    v_scr[WINDOW:WINDOW + tile, :] = v.astype(BF16)
    pk_ref[0] = k[tile - WINDOW:tile, :]
    pv_ref[0] = v[tile - WINDOW:tile, :]
    a_off = Q_COLS + 2 * KV_COLS
    u = z[:, a_off:a_off + CONV_CH] * jax.nn.sigmoid(z[:, a_off + CONV_CH:a_off + 2 * CONV_CH])
    u_scr[0, CONV_HALO:CONV_HALO + tile, :] = u
    pc_ref[0] = u_scr[0, CONV_HALO + tile - CONV_STATE:CONV_HALO + tile, :]
    n_shift_rows = CONV_HALO + tile - SUBLANES
    for b in range(1, SUBLANES):
        u_scr[b, 0:n_shift_rows, :] = u_scr[0, b:b + n_shift_rows, :]

    beta = _vec(vec_ref, ROW_BETA)
    cb, ln_g, ln_b = (_vec(vec_ref, r, CONV_CH) for r in (ROW_CONV_B, ROW_LN_G, ROW_LN_B))
    sink_col = _sink_column(sinks_ref, layer, CHUNK)
    n_keys = WINDOW + CHUNK
    key_j = lax.broadcasted_iota(jnp.int32, (1, n_keys), 1)

    def mixer_piece(c):
        r0 = c * CHUNK
        valid = None
        if r0 < WINDOW:
            valid = jnp.logical_or(key_j >= WINDOW - r0, s_idx > 0)
        a_o = _attention(q_scr[r0:r0 + CHUNK, :], k_scr[r0:r0 + n_keys, :], v_scr[r0:r0 + n_keys, :],
                         sink_col, valid)
        cat_scr[r0:r0 + CHUNK, 0:ATTN_WIDTH] = (a_o * beta[:, 0:ATTN_WIDTH]).astype(BF16)

        def window(j):
            off = CONV_SHIFT + j
            b = off % SUBLANES
            return u_scr[b, r0 + off - b:r0 + off - b + CHUNK, :]
        c_o = _conv_ln_silu(window, cw_ref, cb, ln_g, ln_b)
        cat_scr[r0:r0 + CHUNK, ATTN_WIDTH:] = (c_o * beta[:, ATTN_WIDTH:]).astype(BF16)

    for c in range(n_pieces):
        if c + 1 < n_pieces - 1:
            ffn_piece(c + 1)
        mixer_piece(c)

    k_scr[0:WINDOW, :] = k_scr[tile:tile + WINDOW, :]
    v_scr[0:WINDOW, :] = v_scr[tile:tile + WINDOW, :]
    u_scr[0, 0:CONV_HALO, :] = u_scr[0, tile:tile + CONV_HALO, :]
    x1 = x_ref[0] + _dot(cat_scr[...], wout_ref[...])
    ffn_piece(n_pieces - 1)
    x1_scr[...] = x1
    h2_scr[1 - slot] = _rms_norm(x1, _vec(vec_ref, ROW_G2)).astype(BF16)


def _sample_layer_kernel(sinks_ref, x_ref, ck_ref, cv_ref, st_ref, cos_ref, sp_ref, sn_ref, vec_ref,
                         win_ref, seg_ref, cw_ref, wout_ref, wup_ref, wdn_ref,
                         y_ref, sk_ref, sv_ref, sc_ref,
                         q_scr, kx_scr, vx_scr, ux_scr, cat_scr, *, layer, n_seq, t_new):
    x = x_ref[...]
    h = _rms_norm(x, _vec(vec_ref, ROW_G1)).astype(BF16)
    z = _dot(h, win_ref[...])

    cos, sp, sn = cos_ref[...], sp_ref[...], sn_ref[...]
    seg = seg_ref[...]
    q = _head_norm_rope(z[:, :Q_COLS], _vec(vec_ref, ROW_QG, LANES), seg, cos, sp, sn) * (HEAD_DIM ** -0.5)
    q_scr[...] = q.astype(BF16)
    k = _head_norm_rope(z[:, Q_COLS:Q_COLS + KV_COLS], _vec(vec_ref, ROW_KG, LANES), seg, cos, sp, sn)
    v = z[:, Q_COLS + KV_COLS:Q_COLS + 2 * KV_COLS]
    a_off = Q_COLS + 2 * KV_COLS
    u = z[:, a_off:a_off + CONV_CH] * jax.nn.sigmoid(z[:, a_off + CONV_CH:a_off + 2 * CONV_CH])

    kx_scr[:, 0:WINDOW, :] = ck_ref[...]
    vx_scr[:, 0:WINDOW, :] = cv_ref[...]
    kx_scr[:, WINDOW:WINDOW + t_new, :] = k.reshape(n_seq, t_new, LANES)
    vx_scr[:, WINDOW:WINDOW + t_new, :] = v.reshape(n_seq, t_new, LANES)
    ux_scr[:, CONV_SHIFT:CONV_HALO, :] = st_ref[...]
    ux_scr[:, CONV_HALO:CONV_HALO + t_new, :] = u.reshape(n_seq, t_new, CONV_CH)
    sk_ref[...] = kx_scr[:, t_new:t_new + WINDOW, :]
    sv_ref[...] = vx_scr[:, t_new:t_new + WINDOW, :]
    sc_ref[...] = ux_scr[:, CONV_HALO + t_new - CONV_STATE:CONV_HALO + t_new, :]

    beta = _vec(vec_ref, ROW_BETA)
    cb, ln_g, ln_b = (_vec(vec_ref, r, CONV_CH) for r in (ROW_CONV_B, ROW_LN_G, ROW_LN_B))
    sink_col = _sink_column(sinks_ref, layer, t_new)

    def per_group(i, carry):
        for j in range(SAMPLE_UNROLL):
            b = i * SAMPLE_UNROLL + j
            r0 = pl.multiple_of(b * t_new, t_new)
            a_o = _attention(q_scr[pl.ds(r0, t_new), :], kx_scr[b].astype(BF16), vx_scr[b].astype(BF16),
                             sink_col, None)
            cat_scr[pl.ds(r0, t_new), 0:ATTN_WIDTH] = (a_o * beta[:, 0:ATTN_WIDTH]).astype(BF16)
            window = lambda tap, b=b: ux_scr[b, CONV_SHIFT + tap:CONV_SHIFT + tap + t_new, :]
            c_o = _conv_ln_silu(window, cw_ref, cb, ln_g, ln_b)
            cat_scr[pl.ds(r0, t_new), ATTN_WIDTH:] = (c_o * beta[:, ATTN_WIDTH:]).astype(BF16)
        return carry

    lax.fori_loop(0, n_seq // SAMPLE_UNROLL, per_group, 0)

    x1 = x + _dot(cat_scr[...], wout_ref[...])
    h2 = _rms_norm(x1, _vec(vec_ref, ROW_G2)).astype(BF16)
    y = x1
    for c in range(SAMPLE_FFN_PIECES):
        y = y + _ffn_part(h2, wup_ref, wdn_ref, c * (D_FF // SAMPLE_FFN_PIECES), D_FF // SAMPLE_FFN_PIECES)
    y_ref[...] = y


def _layer_block(layer, *shape):
    return pl.BlockSpec((None,) + shape, lambda *_: (layer,) + (0,) * len(shape),
                        pipeline_mode=pl.Buffered(1))


def _smem():
    return pl.BlockSpec(memory_space=pltpu.SMEM)


def _layer_weight_specs(layer):
    return [
        _layer_block(layer, N_VEC_ROWS, D_MODEL),
        _layer_block(layer, D_MODEL, IN_COLS),
        pl.BlockSpec((LANES, LANES), lambda *_: (0, 0), pipeline_mode=pl.Buffered(1)),
        _layer_block(layer, CONV_WIDTH, CONV_CH),
        _layer_block(layer, D_MODEL, D_MODEL),
        _layer_block(layer, D_MODEL, D_FF),
        _layer_block(layer, D_FF, D_MODEL),
    ]


def _prompt_layer(layer, x, sinks, tables, weights, tile):
    batch, seq, _ = x.shape
    n_s = seq // tile
    n_tiles = batch * n_s
    mixer_tile = lambda t: jnp.minimum(t, n_tiles - 1)
    ffn_tile = lambda t: jnp.maximum(t - 1, 0)
    table_spec = pl.BlockSpec((tile, LANES), lambda t: (mixer_tile(t) % n_s, 0))
    per_batch = lambda rows, cols: pl.BlockSpec((1, rows, cols), lambda t: (mixer_tile(t) // n_s, 0, 0))
    return pl.pallas_call(
        functools.partial(_prompt_layer_kernel, layer=layer, tile=tile, n_s=n_s, n_tiles=n_tiles),
        grid=(n_tiles + 1,),
        in_specs=[_smem(),
                  pl.BlockSpec((1, tile, D_MODEL), lambda t: (mixer_tile(t) // n_s, mixer_tile(t) % n_s, 0)),
                  table_spec, table_spec, table_spec] + _layer_weight_specs(layer),
        out_specs=[pl.BlockSpec((1, tile, D_MODEL), lambda t: (ffn_tile(t) // n_s, ffn_tile(t) % n_s, 0)),
                   per_batch(WINDOW, KV_COLS), per_batch(WINDOW, KV_COLS),
                   per_batch(CONV_STATE, CONV_CH)],
        out_shape=[jax.ShapeDtypeStruct((batch, seq, D_MODEL), F32),
                   jax.ShapeDtypeStruct((batch, WINDOW, KV_COLS), F32),
                   jax.ShapeDtypeStruct((batch, WINDOW, KV_COLS), F32),
                   jax.ShapeDtypeStruct((batch, CONV_STATE, CONV_CH), F32)],
        scratch_shapes=[pltpu.VMEM((tile, Q_COLS), BF16),
                        pltpu.VMEM((WINDOW + tile, LANES), BF16),
                        pltpu.VMEM((WINDOW + tile, LANES), BF16),
                        pltpu.VMEM((SUBLANES, CONV_HALO + tile, CONV_CH), F32),
                        pltpu.VMEM((tile, D_MODEL), BF16),
                        pltpu.VMEM((tile, D_MODEL), F32),
                        pltpu.VMEM((2, tile, D_MODEL), BF16)],
        compiler_params=pltpu.CompilerParams(
            dimension_semantics=("arbitrary",),
            vmem_limit_bytes=VMEM_LIMIT_BYTES),
        name="prompt_layer",
    )(sinks, x, *tables, *weights)


def _sample_layer(layer, x, cache_k, cache_v, state, sinks, tables, weights):
    n_seq, t_new, _ = x.shape
    n_tok = n_seq * t_new
    full = lambda *shape: pl.BlockSpec(shape, lambda i: (0,) * len(shape))
    once = lambda *shape: pl.BlockSpec(shape, lambda i: (0,) * len(shape), pipeline_mode=pl.Buffered(1))
    y, sk, sv, sc = pl.pallas_call(
        functools.partial(_sample_layer_kernel, layer=layer, n_seq=n_seq, t_new=t_new),
        grid=(1,),
        in_specs=[_smem(), once(n_tok, D_MODEL), _layer_block(layer, n_seq, WINDOW, KV_COLS),
                  _layer_block(layer, n_seq, WINDOW, KV_COLS), _layer_block(layer, n_seq, CONV_STATE, CONV_CH),
                  once(n_tok, LANES), once(n_tok, LANES), once(n_tok, LANES)] + _layer_weight_specs(layer),
        out_specs=[full(n_tok, D_MODEL), full(n_seq, WINDOW, KV_COLS), full(n_seq, WINDOW, KV_COLS),
                   full(n_seq, CONV_STATE, CONV_CH)],
        out_shape=[jax.ShapeDtypeStruct((n_tok, D_MODEL), F32),
                   jax.ShapeDtypeStruct((n_seq, WINDOW, KV_COLS), F32),
                   jax.ShapeDtypeStruct((n_seq, WINDOW, KV_COLS), F32),
                   jax.ShapeDtypeStruct((n_seq, CONV_STATE, CONV_CH), F32)],
        scratch_shapes=[pltpu.VMEM((n_tok, Q_COLS), BF16),
                        pltpu.VMEM((n_seq, WINDOW + t_new, LANES), F32),
                        pltpu.VMEM((n_seq, WINDOW + t_new, LANES), F32),
                        pltpu.VMEM((n_seq, CONV_HALO + t_new, CONV_CH), F32),
                        pltpu.VMEM((n_tok, D_MODEL), BF16)],
        compiler_params=pltpu.CompilerParams(
            dimension_semantics=("arbitrary",),
            vmem_limit_bytes=VMEM_LIMIT_BYTES),
        name="sample_layer",
    )(sinks, x.reshape(n_tok, D_MODEL), cache_k, cache_v, state, *tables, *weights)
    return y.reshape(n_seq, t_new, D_MODEL), sk, sv, sc


def _rope_tables(pos):
    half = ROT_DIM // 2
    d = jnp.arange(LANES) % HEAD_DIM
    inv = jnp.power(jnp.float32(ROPE_THETA), -(d % half).astype(F32) * 2.0 / ROT_DIM)
    ang = pos.astype(F32)[:, None] * inv[None, :]
    cos, sin = jnp.cos(ang), jnp.sin(ang)
    d = d[None, :]
    return (jnp.where(d < ROT_DIM, cos, 1.0),
            jnp.where((d >= half) & (d < ROT_DIM), sin, 0.0),
            jnp.where(d < half, -sin, 0.0))


def _slab_order(a, axis):
    shape = a.shape
    heads = a.reshape(shape[:axis] + (N_KV_HEADS, GROUP, HEAD_DIM) + shape[axis + 1:])
    return jnp.swapaxes(heads, axis, axis + 1).reshape(shape)


def _pack_weights(norm1_g, w_in, q_norm_g, k_norm_g, conv_w, conv_b, conv_ln_g, conv_ln_b,
                  beta_attn, beta_conv, w_out, norm2_g, w_up, w_down):
    depth = w_in.shape[0]
    wide = lambda a: jnp.pad(a.astype(F32), ((0, 0), (0, D_MODEL - a.shape[1])))
    rows = [None] * N_VEC_ROWS
    rows[ROW_G1], rows[ROW_G2] = norm1_g.astype(F32), norm2_g.astype(F32)
    rows[ROW_BETA] = jnp.concatenate([_slab_order(beta_attn, 1), beta_conv], axis=1).astype(F32)
    rows[ROW_CONV_B], rows[ROW_LN_G], rows[ROW_LN_B] = wide(conv_b), wide(conv_ln_g), wide(conv_ln_b)
    rows[ROW_QG] = wide(jnp.concatenate([q_norm_g, q_norm_g], axis=1))
    rows[ROW_KG] = wide(jnp.concatenate([k_norm_g, k_norm_g], axis=1))
    vec = jnp.stack(rows, axis=1)
    lane = jnp.arange(LANES) // HEAD_DIM
    seg_ones = (lane[:, None] == lane[None, :]).astype(BF16)
    w_in_s = jnp.concatenate([_slab_order(w_in[:, :, :Q_COLS], 2), w_in[:, :, Q_COLS:]], axis=2)
    w_out_s = jnp.concatenate([_slab_order(w_out[:, :ATTN_WIDTH], 1), w_out[:, ATTN_WIDTH:]], axis=1)
    assert vec.shape == (depth, N_VEC_ROWS, D_MODEL)
    return [vec, w_in_s.astype(BF16), seg_ones, conv_w.astype(F32), w_out_s.astype(BF16),
            w_up.astype(BF16), w_down.astype(BF16)]


def kernel(x_prompt, x_sample, cache_k, cache_v, state_conv, norm1_g, w_in, q_norm_g, k_norm_g,
           attn_sinks, conv_w, conv_b, conv_ln_g, conv_ln_b, beta_attn, beta_conv, w_out,
           norm2_g, w_up, w_down):
    batch, seq, _ = x_prompt.shape
    n_seq, t_new, _ = x_sample.shape
    tile = min(PROMPT_TILE, seq)
    assert seq % tile == 0 and tile % CHUNK == 0 and tile >= WINDOW and D_FF % (tile // CHUNK) == 0
    assert n_seq % SAMPLE_UNROLL == 0
    tables_p = _rope_tables(jnp.arange(seq))
    tables_s = tuple(jnp.tile(t, (n_seq, 1)) for t in _rope_tables(PAST_LEN + jnp.arange(t_new)))
    weights = _pack_weights(norm1_g, w_in, q_norm_g, k_norm_g, conv_w, conv_b, conv_ln_g, conv_ln_b,
                            beta_attn, beta_conv, w_out, norm2_g, w_up, w_down)
    sinks = attn_sinks.astype(F32)
    cache_k = cache_k.reshape(DEPTH, n_seq, WINDOW, KV_COLS)
    cache_v = cache_v.reshape(DEPTH, n_seq, WINDOW, KV_COLS)
    yp, ys = x_prompt, x_sample
    outs = [[] for _ in range(6)]
    for l in range(DEPTH):
        yp, pk, pv, pc = _prompt_layer(l, yp, sinks, tables_p, weights, tile)
        ys, sk, sv, sc = _sample_layer(l, ys, cache_k, cache_v, state_conv, sinks, tables_s, weights)
        kv_shape = lambda a: a.reshape(a.shape[0], WINDOW, N_KV_HEADS, HEAD_DIM)
        for dst, val in zip(outs, (kv_shape(pk), kv_shape(pv), pc, kv_shape(sk), kv_shape(sv), sc)):
            dst.append(val)
    return (yp, ys) + tuple(jnp.stack(o) for o in outs)
```

```python
import functools

import jax
import jax.numpy as jnp
from jax import lax
from jax.experimental import pallas as pl
from jax.experimental.pallas import tpu as pltpu

D_MODEL = 1024
DEPTH = 2
PAST_LEN = 2048
CHUNK = 64
WINDOW = 128
HEAD_DIM = 64
ATTN_WIDTH = D_MODEL // 2
N_HEADS = ATTN_WIDTH // HEAD_DIM
N_KV_HEADS = max(1, N_HEADS // 4)
GROUP = N_HEADS // N_KV_HEADS
CONV_CH = D_MODEL - ATTN_WIDTH
CONV_WIDTH = 31
CONV_STATE = CONV_WIDTH - 1
ROT_DIM = HEAD_DIM // 4
ROPE_THETA = 500000.0
D_FF = 4 * D_MODEL
EPS = 1e-6
Q_COLS = N_HEADS * HEAD_DIM
KV_COLS = N_KV_HEADS * HEAD_DIM
IN_COLS = Q_COLS + 2 * KV_COLS + 2 * CONV_CH

LANES = 128
SUBLANES = 8
N_SLABS = Q_COLS // LANES
CONV_HALO = 32
CONV_SHIFT = CONV_HALO - CONV_STATE
PROMPT_TILE = 512
SAMPLE_UNROLL = 4
SAMPLE_FFN_PIECES = 4
VMEM_LIMIT_BYTES = 56 * 1024 * 1024

ROW_G1, ROW_G2, ROW_BETA, ROW_CONV_B, ROW_LN_G, ROW_LN_B, ROW_QG, ROW_KG = range(8)
N_VEC_ROWS = 8

F32 = jnp.float32
BF16 = jnp.bfloat16

assert N_KV_HEADS == 2 and KV_COLS == LANES and GROUP == N_SLABS


def _dot(a, b):
    return jnp.dot(a, b, preferred_element_type=F32)


def _dot_nt(a, b):
    return lax.dot_general(a, b, (((1,), (1,)), ((), ())), preferred_element_type=F32)


def _rms_norm(x, g):
    ms = jnp.mean(x * x, axis=-1, keepdims=True)
    return x * lax.rsqrt(ms + EPS) * g


def _lane_lo(shape):
    return lax.broadcasted_iota(jnp.int32, shape, len(shape) - 1) % LANES < HEAD_DIM


def _head_norm_rope(z, gain, seg_ones, cos, sin_prev, sin_next):
    out = []
    for i in range(z.shape[1] // LANES):
        zi = z[:, i * LANES:(i + 1) * LANES]
        ssq = _dot((zi * zi).astype(BF16), seg_ones)
        zn = zi * lax.rsqrt(ssq * (1.0 / HEAD_DIM) + EPS) * gain
        prev = pltpu.roll(zn, ROT_DIM // 2, axis=1)
        nxt = pltpu.roll(zn, LANES - ROT_DIM // 2, axis=1)
        out.append(zn * cos + prev * sin_prev + nxt * sin_next)
    return out[0] if len(out) == 1 else jnp.concatenate(out, axis=1)


def _sink_column(sinks_ref, layer, rows):
    n_blocks = 2 * N_SLABS
    blk = lax.broadcasted_iota(jnp.int32, (n_blocks * rows, 1), 0) // rows
    col = jnp.zeros((n_blocks * rows, 1), F32)
    for i in range(N_SLABS):
        for half in range(2):
            col = jnp.where(blk == 2 * i + half, sinks_ref[layer, i + GROUP * half], col)
    return col


def _attention(q, k, v, sink_col, valid):
    rows = q.shape[0]
    lo = _lane_lo((rows, LANES))
    zero = jnp.zeros((rows, LANES), q.dtype)
    parts = []
    for i in range(N_SLABS):
        slab = q[:, i * LANES:(i + 1) * LANES]
        parts += [jnp.where(lo, slab, zero), jnp.where(lo, zero, slab)]
    s = _dot_nt(jnp.concatenate(parts, axis=0), k)
    if valid is not None:
        s = jnp.where(valid, s, -jnp.inf)
    m = jnp.maximum(jnp.max(s, axis=-1, keepdims=True), sink_col)
    p = jnp.exp(s - m)
    denom = jnp.sum(p, axis=-1, keepdims=True) + jnp.exp(sink_col - m)
    o = _dot(p.astype(BF16), v) * (1.0 / denom)
    return jnp.concatenate(
        [jnp.where(lo, o[2 * i * rows:(2 * i + 1) * rows], o[(2 * i + 1) * rows:(2 * i + 2) * rows])
         for i in range(N_SLABS)], axis=1)


def _conv_ln_silu(u_window, cw_ref, cb, ln_g, ln_b):
    acc = u_window(0) * cw_ref[0:1, :]
    for j in range(1, CONV_WIDTH):
        acc = acc + u_window(j) * cw_ref[j:j + 1, :]
    y = acc + cb
    mu = jnp.mean(y, axis=-1, keepdims=True)
    yc = y - mu
    var = jnp.mean(yc * yc, axis=-1, keepdims=True)
    yn = yc * lax.rsqrt(var + EPS) * ln_g + ln_b
    return yn * jax.nn.sigmoid(yn)


def _ffn_part(h2, wup_ref, wdn_ref, c0, width):
    up = jnp.maximum(_dot(h2, wup_ref[:, c0:c0 + width]).astype(BF16), 0.0)
    return _dot(up * up, wdn_ref[c0:c0 + width, :])


def _vec(vec_ref, row, width=D_MODEL):
    return vec_ref[row:row + 1, 0:width]


def _prompt_layer_kernel(sinks_ref, x_ref, cos_ref, sp_ref, sn_ref, vec_ref, win_ref, seg_ref, cw_ref,
                         wout_ref, wup_ref, wdn_ref,
                         y_ref, pk_ref, pv_ref, pc_ref,
                         q_scr, k_scr, v_scr, u_scr, cat_scr, x1_scr, h2_scr,
                         *, layer, tile, n_s, n_tiles):
    t = pl.program_id(0)
    s_idx = jnp.minimum(t, n_tiles - 1) % n_s
    slot = t % 2
    n_pieces = tile // CHUNK
    ffn_width = D_FF // n_pieces

    @pl.when(t == 0)
    def _():
        x1_scr[...] = jnp.zeros(x1_scr.shape, F32)
        h2_scr[...] = jnp.zeros(h2_scr.shape, BF16)

    @pl.when(s_idx == 0)
    def _():
        k_scr[0:WINDOW, :] = jnp.zeros((WINDOW, LANES), BF16)
        v_scr[0:WINDOW, :] = jnp.zeros((WINDOW, LANES), BF16)
        u_scr[0, 0:CONV_HALO, :] = jnp.zeros((CONV_HALO, CONV_CH), F32)

    def ffn_piece(c):
        part = _ffn_part(h2_scr[slot], wup_ref, wdn_ref, c * ffn_width, ffn_width)
        if c == 0:
            y_ref[0] = x1_scr[...] + part
        else:
            y_ref[0] += part

    ffn_piece(0)

    h = _rms_norm(x_ref[0], _vec(vec_ref, ROW_G1)).astype(BF16)
    z = _dot(h, win_ref[...])
    cos, sp, sn = cos_ref[...], sp_ref[...], sn_ref[...]
    seg = seg_ref[...]
    q = _head_norm_rope(z[:, :Q_COLS], _vec(vec_ref, ROW_QG, LANES), seg, cos, sp, sn) * (HEAD_DIM ** -0.5)
    q_scr[...] = q.astype(BF16)
    k = _head_norm_rope(z[:, Q_COLS:Q_COLS + KV_COLS], _vec(vec_ref, ROW_KG, LANES), seg, cos, sp, sn)
    v = z[:, Q_COLS + KV_COLS:Q_COLS + 2 * KV_COLS]
    k_scr[WINDOW:WINDOW + tile, :] = k.astype(BF16)
    v_scr[WINDOW:WINDOW + tile, :] = v.astype(BF16)
    pk_ref[0] = k[tile - WINDOW:tile, :]
    pv_ref[0] = v[tile - WINDOW:tile, :]
    a_off = Q_COLS + 2 * KV_COLS
    u = z[:, a_off:a_off + CONV_CH] * jax.nn.sigmoid(z[:, a_off + CONV_CH:a_off + 2 * CONV_CH])
    u_scr[0, CONV_HALO:CONV_HALO + tile, :] = u
    pc_ref[0] = u_scr[0, CONV_HALO + tile - CONV_STATE:CONV_HALO + tile, :]
    n_shift_rows = CONV_HALO + tile - SUBLANES
    for b in range(1, SUBLANES):
        u_scr[b, 0:n_shift_rows, :] = u_scr[0, b:b + n_shift_rows, :]

    beta = _vec(vec_ref, ROW_BETA)
    cb, ln_g, ln_b = (_vec(vec_ref, r, CONV_CH) for r in (ROW_CONV_B, ROW_LN_G, ROW_LN_B))
    sink_col = _sink_column(sinks_ref, layer, CHUNK)
    n_keys = WINDOW + CHUNK
    key_j = lax.broadcasted_iota(jnp.int32, (1, n_keys), 1)

    def mixer_piece(c):
        r0 = c * CHUNK
        valid = None
        if r0 < WINDOW:
            valid = jnp.logical_or(key_j >= WINDOW - r0, s_idx > 0)
        a_o = _attention(q_scr[r0:r0 + CHUNK, :], k_scr[r0:r0 + n_keys, :], v_scr[r0:r0 + n_keys, :],
                         sink_col, valid)
        cat_scr[r0:r0 + CHUNK, 0:ATTN_WIDTH] = (a_o * beta[:, 0:ATTN_WIDTH]).astype(BF16)

        def window(j):
            off = CONV_SHIFT + j
            b = off % SUBLANES
            return u_scr[b, r0 + off - b:r0 + off - b + CHUNK, :]
        c_o = _conv_ln_silu(window, cw_ref, cb, ln_g, ln_b)
        cat_scr[r0:r0 + CHUNK, ATTN_WIDTH:] = (c_o * beta[:, ATTN_WIDTH:]).astype(BF16)

    for c in range(n_pieces):
        if c + 1 < n_pieces - 1:
            ffn_piece(c + 1)
        mixer_piece(c)

    k_scr[0:WINDOW, :] = k_scr[tile:tile + WINDOW, :]
    v_scr[0:WINDOW, :] = v_scr[tile:tile + WINDOW, :]
    u_scr[0, 0:CONV_HALO, :] = u_scr[0, tile:tile + CONV_HALO, :]
    x1 = x_ref[0] + _dot(cat_scr[...], wout_ref[...])
    ffn_piece(n_pieces - 1)
    x1_scr[...] = x1
    h2_scr[1 - slot] = _rms_norm(x1, _vec(vec_ref, ROW_G2)).astype(BF16)


def _sample_layer_kernel(sinks_ref, x_ref, ck_ref, cv_ref, st_ref, cos_ref, sp_ref, sn_ref, vec_ref,
                         win_ref, seg_ref, cw_ref, wout_ref, wup_ref, wdn_ref,
                         y_ref, sk_ref, sv_ref, sc_ref,
                         q_scr, kx_scr, vx_scr, ux_scr, cat_scr, *, layer, n_seq, t_new):
    x = x_ref[...]
    h = _rms_norm(x, _vec(vec_ref, ROW_G1)).astype(BF16)
    z = _dot(h, win_ref[...])

    cos, sp, sn = cos_ref[...], sp_ref[...], sn_ref[...]
    seg = seg_ref[...]
    q = _head_norm_rope(z[:, :Q_COLS], _vec(vec_ref, ROW_QG, LANES), seg, cos, sp, sn) * (HEAD_DIM ** -0.5)
    q_scr[...] = q.astype(BF16)
    k = _head_norm_rope(z[:, Q_COLS:Q_COLS + KV_COLS], _vec(vec_ref, ROW_KG, LANES), seg, cos, sp, sn)
    v = z[:, Q_COLS + KV_COLS:Q_COLS + 2 * KV_COLS]
    a_off = Q_COLS + 2 * KV_COLS
    u = z[:, a_off:a_off + CONV_CH] * jax.nn.sigmoid(z[:, a_off + CONV_CH:a_off + 2 * CONV_CH])

    kx_scr[:, 0:WINDOW, :] = ck_ref[...]
    vx_scr[:, 0:WINDOW, :] = cv_ref[...]
    kx_scr[:, WINDOW:WINDOW + t_new, :] = k.reshape(n_seq, t_new, LANES)
    vx_scr[:, WINDOW:WINDOW + t_new, :] = v.reshape(n_seq, t_new, LANES)
    ux_scr[:, CONV_SHIFT:CONV_HALO, :] = st_ref[...]
    ux_scr[:, CONV_HALO:CONV_HALO + t_new, :] = u.reshape(n_seq, t_new, CONV_CH)
    sk_ref[...] = kx_scr[:, t_new:t_new + WINDOW, :]
    sv_ref[...] = vx_scr[:, t_new:t_new + WINDOW, :]
    sc_ref[...] = ux_scr[:, CONV_HALO + t_new - CONV_STATE:CONV_HALO + t_new, :]

    beta = _vec(vec_ref, ROW_BETA)
    cb, ln_g, ln_b = (_vec(vec_ref, r, CONV_CH) for r in (ROW_CONV_B, ROW_LN_G, ROW_LN_B))
    sink_col = _sink_column(sinks_ref, layer, t_new)

    def per_group(i, carry):
        for j in range(SAMPLE_UNROLL):
            b = i * SAMPLE_UNROLL + j
            r0 = pl.multiple_of(b * t_new, t_new)
            a_o = _attention(q_scr[pl.ds(r0, t_new), :], kx_scr[b].astype(BF16), vx_scr[b].astype(BF16),
                             sink_col, None)
            cat_scr[pl.ds(r0, t_new), 0:ATTN_WIDTH] = (a_o * beta[:, 0:ATTN_WIDTH]).astype(BF16)
            window = lambda tap, b=b: ux_scr[b, CONV_SHIFT + tap:CONV_SHIFT + tap + t_new, :]
            c_o = _conv_ln_silu(window, cw_ref, cb, ln_g, ln_b)
            cat_scr[pl.ds(r0, t_new), ATTN_WIDTH:] = (c_o * beta[:, ATTN_WIDTH:]).astype(BF16)
        return carry

    lax.fori_loop(0, n_seq // SAMPLE_UNROLL, per_group, 0)

    x1 = x + _dot(cat_scr[...], wout_ref[...])
    h2 = _rms_norm(x1, _vec(vec_ref, ROW_G2)).astype(BF16)
    y = x1
    for c in range(SAMPLE_FFN_PIECES):
        y = y + _ffn_part(h2, wup_ref, wdn_ref, c * (D_FF // SAMPLE_FFN_PIECES), D_FF // SAMPLE_FFN_PIECES)
    y_ref[...] = y


def _layer_block(layer, *shape):
    return pl.BlockSpec((None,) + shape, lambda *_: (layer,) + (0,) * len(shape),
                        pipeline_mode=pl.Buffered(1))


def _smem():
    return pl.BlockSpec(memory_space=pltpu.SMEM)


def _layer_weight_specs(layer):
    return [
        _layer_block(layer, N_VEC_ROWS, D_MODEL),
        _layer_block(layer, D_MODEL, IN_COLS),
        pl.BlockSpec((LANES, LANES), lambda *_: (0, 0), pipeline_mode=pl.Buffered(1)),
        _layer_block(layer, CONV_WIDTH, CONV_CH),
        _layer_block(layer, D_MODEL, D_MODEL),
        _layer_block(layer, D_MODEL, D_FF),
        _layer_block(layer, D_FF, D_MODEL),
    ]


def _prompt_layer(layer, x, sinks, tables, weights, tile):
    batch, seq, _ = x.shape
    n_s = seq // tile
    n_tiles = batch * n_s
    mixer_tile = lambda t: jnp.minimum(t, n_tiles - 1)
    ffn_tile = lambda t: jnp.maximum(t - 1, 0)
    table_spec = pl.BlockSpec((tile, LANES), lambda t: (mixer_tile(t) % n_s, 0))
    per_batch = lambda rows, cols: pl.BlockSpec((1, rows, cols), lambda t: (mixer_tile(t) // n_s, 0, 0))
    return pl.pallas_call(
        functools.partial(_prompt_layer_kernel, layer=layer, tile=tile, n_s=n_s, n_tiles=n_tiles),
        grid=(n_tiles + 1,),
        in_specs=[_smem(),
                  pl.BlockSpec((1, tile, D_MODEL), lambda t: (mixer_tile(t) // n_s, mixer_tile(t) % n_s, 0)),
                  table_spec, table_spec, table_spec] + _layer_weight_specs(layer),
        out_specs=[pl.BlockSpec((1, tile, D_MODEL), lambda t: (ffn_tile(t) // n_s, ffn_tile(t) % n_s, 0)),
                   per_batch(WINDOW, KV_COLS), per_batch(WINDOW, KV_COLS),
                   per_batch(CONV_STATE, CONV_CH)],
        out_shape=[jax.ShapeDtypeStruct((batch, seq, D_MODEL), F32),
                   jax.ShapeDtypeStruct((batch, WINDOW, KV_COLS), F32),
                   jax.ShapeDtypeStruct((batch, WINDOW, KV_COLS), F32),
                   jax.ShapeDtypeStruct((batch, CONV_STATE, CONV_CH), F32)],
        scratch_shapes=[pltpu.VMEM((tile, Q_COLS), BF16),
                        pltpu.VMEM((WINDOW + tile, LANES), BF16),
                        pltpu.VMEM((WINDOW + tile, LANES), BF16),
                        pltpu.VMEM((SUBLANES, CONV_HALO + tile, CONV_CH), F32),
                        pltpu.VMEM((tile, D_MODEL), BF16),
                        pltpu.VMEM((tile, D_MODEL), F32),
                        pltpu.VMEM((2, tile, D_MODEL), BF16)],
        compiler_params=pltpu.CompilerParams(
            dimension_semantics=("arbitrary",),
            vmem_limit_bytes=VMEM_LIMIT_BYTES),
        name="prompt_layer",
    )(sinks, x, *tables, *weights)


def _sample_layer(layer, x, cache_k, cache_v, state, sinks, tables, weights):
    n_seq, t_new, _ = x.shape
    n_tok = n_seq * t_new
    full = lambda *shape: pl.BlockSpec(shape, lambda i: (0,) * len(shape))
    once = lambda *shape: pl.BlockSpec(shape, lambda i: (0,) * len(shape), pipeline_mode=pl.Buffered(1))
    y, sk, sv, sc = pl.pallas_call(
        functools.partial(_sample_layer_kernel, layer=layer, n_seq=n_seq, t_new=t_new),
        grid=(1,),
        in_specs=[_smem(), once(n_tok, D_MODEL), _layer_block(layer, n_seq, WINDOW, KV_COLS),
                  _layer_block(layer, n_seq, WINDOW, KV_COLS), _layer_block(layer, n_seq, CONV_STATE, CONV_CH),
                  once(n_tok, LANES), once(n_tok, LANES), once(n_tok, LANES)] + _layer_weight_specs(layer),
        out_specs=[full(n_tok, D_MODEL), full(n_seq, WINDOW, KV_COLS), full(n_seq, WINDOW, KV_COLS),
                   full(n_seq, CONV_STATE, CONV_CH)],
        out_shape=[jax.ShapeDtypeStruct((n_tok, D_MODEL), F32),
                   jax.ShapeDtypeStruct((n_seq, WINDOW, KV_COLS), F32),
                   jax.ShapeDtypeStruct((n_seq, WINDOW, KV_COLS), F32),
                   jax.ShapeDtypeStruct((n_seq, CONV_STATE, CONV_CH), F32)],
        scratch_shapes=[pltpu.VMEM((n_tok, Q_COLS), BF16),
                        pltpu.VMEM((n_seq, WINDOW + t_new, LANES), F32),
                        pltpu.VMEM((n_seq, WINDOW + t_new, LANES), F32),
                        pltpu.VMEM((n_seq, CONV_HALO + t_new, CONV_CH), F32),
                        pltpu.VMEM((n_tok, D_MODEL), BF16)],
        compiler_params=pltpu.CompilerParams(
            dimension_semantics=("arbitrary",),
            vmem_limit_bytes=VMEM_LIMIT_BYTES),
        name="sample_layer",
    )(sinks, x.reshape(n_tok, D_MODEL), cache_k, cache_v, state, *tables, *weights)
    return y.reshape(n_seq, t_new, D_MODEL), sk, sv, sc


def _rope_tables(pos):
    half = ROT_DIM // 2
    d = jnp.arange(LANES) % HEAD_DIM
    inv = jnp.power(jnp.float32(ROPE_THETA), -(d % half).astype(F32) * 2.0 / ROT_DIM)
    ang = pos.astype(F32)[:, None] * inv[None, :]
    cos, sin = jnp.cos(ang), jnp.sin(ang)
    d = d[None, :]
    return (jnp.where(d < ROT_DIM, cos, 1.0),
            jnp.where((d >= half) & (d < ROT_DIM), sin, 0.0),
            jnp.where(d < half, -sin, 0.0))


def _slab_order(a, axis):
    shape = a.shape
    heads = a.reshape(shape[:axis] + (N_KV_HEADS, GROUP, HEAD_DIM) + shape[axis + 1:])
    return jnp.swapaxes(heads, axis, axis + 1).reshape(shape)


def _pack_weights(norm1_g, w_in, q_norm_g, k_norm_g, conv_w, conv_b, conv_ln_g, conv_ln_b,
                  beta_attn, beta_conv, w_out, norm2_g, w_up, w_down):
    depth = w_in.shape[0]
    wide = lambda a: jnp.pad(a.astype(F32), ((0, 0), (0, D_MODEL - a.shape[1])))
    rows = [None] * N_VEC_ROWS
    rows[ROW_G1], rows[ROW_G2] = norm1_g.astype(F32), norm2_g.astype(F32)
    rows[ROW_BETA] = jnp.concatenate([_slab_order(beta_attn, 1), beta_conv], axis=1).astype(F32)
    rows[ROW_CONV_B], rows[ROW_LN_G], rows[ROW_LN_B] = wide(conv_b), wide(conv_ln_g), wide(conv_ln_b)
    rows[ROW_QG] = wide(jnp.concatenate([q_norm_g, q_norm_g], axis=1))
    rows[ROW_KG] = wide(jnp.concatenate([k_norm_g, k_norm_g], axis=1))
    vec = jnp.stack(rows, axis=1)
    lane = jnp.arange(LANES) // HEAD_DIM
    seg_ones = (lane[:, None] == lane[None, :]).astype(BF16)
    w_in_s = jnp.concatenate([_slab_order(w_in[:, :, :Q_COLS], 2), w_in[:, :, Q_COLS:]], axis=2)
    w_out_s = jnp.concatenate([_slab_order(w_out[:, :ATTN_WIDTH], 1), w_out[:, ATTN_WIDTH:]], axis=1)
    assert vec.shape == (depth, N_VEC_ROWS, D_MODEL)
    return [vec, w_in_s.astype(BF16), seg_ones, conv_w.astype(F32), w_out_s.astype(BF16),
            w_up.astype(BF16), w_down.astype(BF16)]


def kernel(x_prompt, x_sample, cache_k, cache_v, state_conv, norm1_g, w_in, q_norm_g, k_norm_g,
           attn_sinks, conv_w, conv_b, conv_ln_g, conv_ln_b, beta_attn, beta_conv, w_out,
           norm2_g, w_up, w_down):
    batch, seq, _ = x_prompt.shape
    n_seq, t_new, _ = x_sample.shape
    tile = min(PROMPT_TILE, seq)
    assert seq % tile == 0 and tile % CHUNK == 0 and tile >= WINDOW and D_FF % (tile // CHUNK) == 0
    assert n_seq % SAMPLE_UNROLL == 0
    tables_p = _rope_tables(jnp.arange(seq))
    tables_s = tuple(jnp.tile(t, (n_seq, 1)) for t in _rope_tables(PAST_LEN + jnp.arange(t_new)))
    weights = _pack_weights(norm1_g, w_in, q_norm_g, k_norm_g, conv_w, conv_b, conv_ln_g, conv_ln_b,
                            beta_attn, beta_conv, w_out, norm2_g, w_up, w_down)
    sinks = attn_sinks.astype(F32)
    cache_k = cache_k.reshape(DEPTH, n_seq, WINDOW, KV_COLS)
    cache_v = cache_v.reshape(DEPTH, n_seq, WINDOW, KV_COLS)
    yp, ys = x_prompt, x_sample
    outs = [[] for _ in range(6)]
    for l in range(DEPTH):
        yp, pk, pv, pc = _prompt_layer(l, yp, sinks, tables_p, weights, tile)
        ys, sk, sv, sc = _sample_layer(l, ys, cache_k, cache_v, state_conv, sinks, tables_s, weights)
        kv_shape = lambda a: a.reshape(a.shape[0], WINDOW, N_KV_HEADS, HEAD_DIM)
        for dst, val in zip(outs, (kv_shape(pk), kv_shape(pv), pc, kv_shape(sk), kv_shape(sv), sc)):
            dst.append(val)
    return (yp, ys) + tuple(jnp.stack(o) for o in outs)
```

```python
import functools

import jax
import jax.numpy as jnp
from jax import lax
from jax.experimental import pallas as pl
from jax.experimental.pallas import tpu as pltpu

D_MODEL = 1024
DEPTH = 2
PAST_LEN = 2048
CHUNK = 64
WINDOW = 128
HEAD_DIM = 64
ATTN_WIDTH = D_MODEL // 2
N_HEADS = ATTN_WIDTH // HEAD_DIM
N_KV_HEADS = max(1, N_HEADS // 4)
GROUP = N_HEADS // N_KV_HEADS
CONV_CH = D_MODEL - ATTN_WIDTH
CONV_WIDTH = 31
CONV_STATE = CONV_WIDTH - 1
ROT_DIM = HEAD_DIM // 4
ROPE_THETA = 500000.0
D_FF = 4 * D_MODEL
EPS = 1e-6
Q_COLS = N_HEADS * HEAD_DIM
KV_COLS = N_KV_HEADS * HEAD_DIM
IN_COLS = Q_COLS + 2 * KV_COLS + 2 * CONV_CH

LANES = 128
SUBLANES = 8
N_SLABS = Q_COLS // LANES
CONV_HALO = 32
CONV_SHIFT = CONV_HALO - CONV_STATE
PROMPT_TILE = 512
SAMPLE_UNROLL = 4
SAMPLE_FFN_PIECES = 4
VMEM_LIMIT_BYTES = 56 * 1024 * 1024

ROW_G1, ROW_G2, ROW_BETA, ROW_CONV_B, ROW_LN_G, ROW_LN_B, ROW_QG, ROW_KG = range(8)
N_VEC_ROWS = 8

LOG2_E = 1.4426950408889634
Q_SCALE = HEAD_DIM ** -0.5 * LOG2_E

F32 = jnp.float32
BF16 = jnp.bfloat16

assert N_KV_HEADS == 2 and KV_COLS == LANES and GROUP == N_SLABS


def _dot(a, b):
    return jnp.dot(a, b, preferred_element_type=F32)


def _dot_nt(a, b):
    return lax.dot_general(a, b, (((1,), (1,)), ((), ())), preferred_element_type=F32)


def _rms_norm(x, g):
    ms = jnp.mean(x * x, axis=-1, keepdims=True)
    return x * lax.rsqrt(ms + EPS) * g


def _lane_lo(shape):
    return lax.broadcasted_iota(jnp.int32, shape, len(shape) - 1) % LANES < HEAD_DIM


def _head_norm_rope(z, gain, seg_ones, cos, sin_prev, sin_next):
    out = []
    for i in range(z.shape[1] // LANES):
        zi = z[:, i * LANES:(i + 1) * LANES]
        ssq = _dot((zi * zi).astype(BF16), seg_ones)
        zn = zi * lax.rsqrt(ssq * (1.0 / HEAD_DIM) + EPS) * gain
        prev = pltpu.roll(zn, ROT_DIM // 2, axis=1)
        nxt = pltpu.roll(zn, LANES - ROT_DIM // 2, axis=1)
        out.append(zn * cos + prev * sin_prev + nxt * sin_next)
    return out[0] if len(out) == 1 else jnp.concatenate(out, axis=1)


def _sink_column(sinks_ref, layer, rows):
    n_blocks = 2 * N_SLABS
    blk = lax.broadcasted_iota(jnp.int32, (n_blocks * rows, 1), 0) // rows
    col = jnp.zeros((n_blocks * rows, 1), F32)
    for i in range(N_SLABS):
        for half in range(2):
            col = jnp.where(blk == 2 * i + half, sinks_ref[layer, i + GROUP * half] * LOG2_E, col)
    return col


def _attention(q, k, v, sink_col, valid):
    rows = q.shape[0]
    lo = _lane_lo((rows, LANES))
    zero = jnp.zeros((rows, LANES), q.dtype)
    parts = []
    for i in range(N_SLABS):
        slab = q[:, i * LANES:(i + 1) * LANES]
        parts += [jnp.where(lo, slab, zero), jnp.where(lo, zero, slab)]
    s = _dot_nt(jnp.concatenate(parts, axis=0), k)
    if valid is not None:
        s = jnp.where(valid, s, -jnp.inf)
    m = jnp.maximum(jnp.max(s, axis=-1, keepdims=True), sink_col)
    p = jnp.exp2(s - m)
    denom = jnp.sum(p, axis=-1, keepdims=True) + jnp.exp2(sink_col - m)
    o = _dot(p.astype(BF16), v) * (1.0 / denom)
    return jnp.concatenate(
        [jnp.where(lo, o[2 * i * rows:(2 * i + 1) * rows], o[(2 * i + 1) * rows:(2 * i + 2) * rows])
         for i in range(N_SLABS)], axis=1)


def _conv_ln_silu(u_window, cw_ref, cb, ln_g, ln_b):
    def tap(j):
        u = u_window(j)
        return u.reshape(u.shape[0] // SUBLANES, SUBLANES, u.shape[1]) * cw_ref[j][None]
    acc = tap(0)
    for j in range(1, CONV_WIDTH):
        acc = acc + tap(j)
    y = acc.reshape(acc.shape[0] * SUBLANES, acc.shape[2]) + cb
    mu = jnp.mean(y, axis=-1, keepdims=True)
    yc = y - mu
    var = jnp.mean(yc * yc, axis=-1, keepdims=True)
    yn = yc * lax.rsqrt(var + EPS) * ln_g + ln_b
    return yn * jax.nn.sigmoid(yn)


def _ffn_part(h2, wup_ref, wdn_ref, c0, width):
    up = jnp.maximum(_dot(h2, wup_ref[:, c0:c0 + width]).astype(BF16), 0.0)
    return _dot(up * up, wdn_ref[c0:c0 + width, :])


def _vec(vec_ref, row, width=D_MODEL):
    return vec_ref[row:row + 1, 0:width]


def _prompt_layer_kernel(sinks_ref, x_ref, cos_ref, sp_ref, sn_ref, vec_ref, win_ref, seg_ref, cw_ref,
                         wout_ref, wup_ref, wdn_ref,
                         y_ref, pk_ref, pv_ref, pc_ref,
                         q_scr, k_scr, v_scr, u_scr, cat_scr, x1_scr, h2_scr,
                         *, layer, tile, n_s, n_tiles):
    t = pl.program_id(0)
    s_idx = jnp.minimum(t, n_tiles - 1) % n_s
    slot = t % 2
    n_pieces = tile // CHUNK
    ffn_width = D_FF // n_pieces

    @pl.when(t == 0)
    def _():
        x1_scr[...] = jnp.zeros(x1_scr.shape, F32)
        h2_scr[...] = jnp.zeros(h2_scr.shape, BF16)

    @pl.when(s_idx == 0)
    def _():
        k_scr[0:WINDOW, :] = jnp.zeros((WINDOW, LANES), BF16)
        v_scr[0:WINDOW, :] = jnp.zeros((WINDOW, LANES), BF16)
        u_scr[0, 0:CONV_HALO, :] = jnp.zeros((CONV_HALO, CONV_CH), F32)

    def ffn_piece(c):
        part = _ffn_part(h2_scr[slot], wup_ref, wdn_ref, c * ffn_width, ffn_width)
        if c == 0:
            y_ref[0] = x1_scr[...] + part
        else:
            y_ref[0] += part

    ffn_piece(0)

    h = _rms_norm(x_ref[0], _vec(vec_ref, ROW_G1)).astype(BF16)
    z = _dot(h, win_ref[...])
    cos, sp, sn = cos_ref[...], sp_ref[...], sn_ref[...]
    seg = seg_ref[...]
    q = _head_norm_rope(z[:, :Q_COLS], _vec(vec_ref, ROW_QG, LANES), seg, cos, sp, sn) * Q_SCALE
    q_scr[...] = q.astype(BF16)
    k = _head_norm_rope(z[:, Q_COLS:Q_COLS + KV_COLS], _vec(vec_ref, ROW_KG, LANES), seg, cos, sp, sn)
    v = z[:, Q_COLS + KV_COLS:Q_COLS + 2 * KV_COLS]
    k_scr[WINDOW:WINDOW + tile, :] = k.astype(BF16)
    v_scr[WINDOW:WINDOW + tile, :] = v.astype(BF16)
    pk_ref[0] = k[tile - WINDOW:tile, :]
    pv_ref[0] = v[tile - WINDOW:tile, :]
    a_off = Q_COLS + 2 * KV_COLS
    u = z[:, a_off:a_off + CONV_CH] * jax.nn.sigmoid(z[:, a_off + CONV_CH:a_off + 2 * CONV_CH])
    u_scr[0, CONV_HALO:CONV_HALO + tile, :] = u
    pc_ref[0] = u_scr[0, CONV_HALO + tile - CONV_STATE:CONV_HALO + tile, :]
    n_shift_rows = CONV_HALO + tile - SUBLANES
    n_groups = (CONV_HALO + tile) // SUBLANES
    u_groups = u_scr[0].reshape(n_groups, SUBLANES, CONV_CH)
    row_in_group = lax.broadcasted_iota(jnp.int32, (n_groups - 1, SUBLANES, CONV_CH), 1)
    for b in range(1, SUBLANES):
        rot = pltpu.roll(u_groups, SUBLANES - b, axis=1)
        shifted = jnp.where(row_in_group < SUBLANES - b, rot[:-1], rot[1:])
        u_scr[b, 0:n_shift_rows, :] = shifted.reshape(n_shift_rows, CONV_CH)

    beta = _vec(vec_ref, ROW_BETA)
    cb, ln_g, ln_b = (_vec(vec_ref, r, CONV_CH) for r in (ROW_CONV_B, ROW_LN_G, ROW_LN_B))
    sink_col = _sink_column(sinks_ref, layer, CHUNK)
    n_keys = WINDOW + CHUNK
    key_j = lax.broadcasted_iota(jnp.int32, (1, n_keys), 1)

    def mixer_piece(c):
        r0 = c * CHUNK
        valid = None
        if r0 < WINDOW:
            valid = jnp.logical_or(key_j >= WINDOW - r0, s_idx > 0)
        a_o = _attention(q_scr[r0:r0 + CHUNK, :], k_scr[r0:r0 + n_keys, :], v_scr[r0:r0 + n_keys, :],
                         sink_col, valid)
        cat_scr[r0:r0 + CHUNK, 0:ATTN_WIDTH] = (a_o * beta[:, 0:ATTN_WIDTH]).astype(BF16)

        def window(j):
            off = CONV_SHIFT + j
            b = off % SUBLANES
            return u_scr[b, r0 + off - b:r0 + off - b + CHUNK, :]
        c_o = _conv_ln_silu(window, cw_ref, cb, ln_g, ln_b)
        cat_scr[r0:r0 + CHUNK, ATTN_WIDTH:] = (c_o * beta[:, ATTN_WIDTH:]).astype(BF16)

    for c in range(n_pieces):
        if c + 1 < n_pieces - 1:
            ffn_piece(c + 1)
        mixer_piece(c)

    k_scr[0:WINDOW, :] = k_scr[tile:tile + WINDOW, :]
    v_scr[0:WINDOW, :] = v_scr[tile:tile + WINDOW, :]
    u_scr[0, 0:CONV_HALO, :] = u_scr[0, tile:tile + CONV_HALO, :]
    x1 = x_ref[0] + _dot(cat_scr[...], wout_ref[...])
    ffn_piece(n_pieces - 1)
    x1_scr[...] = x1
    h2_scr[1 - slot] = _rms_norm(x1, _vec(vec_ref, ROW_G2)).astype(BF16)


def _sample_layer_kernel(sinks_ref, x_ref, ck_ref, cv_ref, st_ref, cos_ref, sp_ref, sn_ref, vec_ref,
                         win_ref, seg_ref, cw_ref, wout_ref, wup_ref, wdn_ref,
                         y_ref, sk_ref, sv_ref, sc_ref,
                         q_scr, kx_scr, vx_scr, ux_scr, cat_scr, *, layer, n_seq, t_new):
    x = x_ref[...]
    h = _rms_norm(x, _vec(vec_ref, ROW_G1)).astype(BF16)
    z = _dot(h, win_ref[...])

    cos, sp, sn = cos_ref[...], sp_ref[...], sn_ref[...]
    seg = seg_ref[...]
    q = _head_norm_rope(z[:, :Q_COLS], _vec(vec_ref, ROW_QG, LANES), seg, cos, sp, sn) * Q_SCALE
    q_scr[...] = q.astype(BF16)
    k = _head_norm_rope(z[:, Q_COLS:Q_COLS + KV_COLS], _vec(vec_ref, ROW_KG, LANES), seg, cos, sp, sn)
    v = z[:, Q_COLS + KV_COLS:Q_COLS + 2 * KV_COLS]
    a_off = Q_COLS + 2 * KV_COLS
    u = z[:, a_off:a_off + CONV_CH] * jax.nn.sigmoid(z[:, a_off + CONV_CH:a_off + 2 * CONV_CH])

    kx_scr[:, 0:WINDOW, :] = ck_ref[...]
    vx_scr[:, 0:WINDOW, :] = cv_ref[...]
    kx_scr[:, WINDOW:WINDOW + t_new, :] = k.reshape(n_seq, t_new, LANES)
    vx_scr[:, WINDOW:WINDOW + t_new, :] = v.reshape(n_seq, t_new, LANES)
    ux_scr[:, CONV_SHIFT:CONV_HALO, :] = st_ref[...]
    ux_scr[:, CONV_HALO:CONV_HALO + t_new, :] = u.reshape(n_seq, t_new, CONV_CH)
    sk_ref[...] = kx_scr[:, t_new:t_new + WINDOW, :]
    sv_ref[...] = vx_scr[:, t_new:t_new + WINDOW, :]
    sc_ref[...] = ux_scr[:, CONV_HALO + t_new - CONV_STATE:CONV_HALO + t_new, :]

    beta = _vec(vec_ref, ROW_BETA)
    cb, ln_g, ln_b = (_vec(vec_ref, r, CONV_CH) for r in (ROW_CONV_B, ROW_LN_G, ROW_LN_B))
    sink_col = _sink_column(sinks_ref, layer, t_new)

    def per_group(i, carry):
        for j in range(SAMPLE_UNROLL):
            b = i * SAMPLE_UNROLL + j
            r0 = pl.multiple_of(b * t_new, t_new)
            a_o = _attention(q_scr[pl.ds(r0, t_new), :], kx_scr[b].astype(BF16), vx_scr[b].astype(BF16),
                             sink_col, None)
            cat_scr[pl.ds(r0, t_new), 0:ATTN_WIDTH] = (a_o * beta[:, 0:ATTN_WIDTH]).astype(BF16)
            window = lambda tap, b=b: ux_scr[b, CONV_SHIFT + tap:CONV_SHIFT + tap + t_new, :]
            c_o = _conv_ln_silu(window, cw_ref, cb, ln_g, ln_b)
            cat_scr[pl.ds(r0, t_new), ATTN_WIDTH:] = (c_o * beta[:, ATTN_WIDTH:]).astype(BF16)
        return carry

    lax.fori_loop(0, n_seq // SAMPLE_UNROLL, per_group, 0)

    x1 = x + _dot(cat_scr[...], wout_ref[...])
    h2 = _rms_norm(x1, _vec(vec_ref, ROW_G2)).astype(BF16)
    y = x1
    for c in range(SAMPLE_FFN_PIECES):
        y = y + _ffn_part(h2, wup_ref, wdn_ref, c * (D_FF // SAMPLE_FFN_PIECES), D_FF // SAMPLE_FFN_PIECES)
    y_ref[...] = y


def _layer_block(layer, *shape):
    return pl.BlockSpec((None,) + shape, lambda *_: (layer,) + (0,) * len(shape),
                        pipeline_mode=pl.Buffered(1))


def _smem():
    return pl.BlockSpec(memory_space=pltpu.SMEM)


def _layer_weight_specs(layer):
    return [
        _layer_block(layer, N_VEC_ROWS, D_MODEL),
        _layer_block(layer, D_MODEL, IN_COLS),
        pl.BlockSpec((LANES, LANES), lambda *_: (0, 0), pipeline_mode=pl.Buffered(1)),
        _layer_block(layer, CONV_WIDTH, SUBLANES, CONV_CH),
        _layer_block(layer, D_MODEL, D_MODEL),
        _layer_block(layer, D_MODEL, D_FF),
        _layer_block(layer, D_FF, D_MODEL),
    ]


def _prompt_layer(layer, x, sinks, tables, weights, tile):
    batch, seq, _ = x.shape
    n_s = seq // tile
    n_tiles = batch * n_s
    mixer_tile = lambda t: jnp.minimum(t, n_tiles - 1)
    ffn_tile = lambda t: jnp.maximum(t - 1, 0)
    table_spec = pl.BlockSpec((tile, LANES), lambda t: (mixer_tile(t) % n_s, 0))
    per_batch = lambda rows, cols: pl.BlockSpec((1, rows, cols), lambda t: (mixer_tile(t) // n_s, 0, 0))
    return pl.pallas_call(
        functools.partial(_prompt_layer_kernel, layer=layer, tile=tile, n_s=n_s, n_tiles=n_tiles),
        grid=(n_tiles + 1,),
        in_specs=[_smem(),
                  pl.BlockSpec((1, tile, D_MODEL), lambda t: (mixer_tile(t) // n_s, mixer_tile(t) % n_s, 0)),
                  table_spec, table_spec, table_spec] + _layer_weight_specs(layer),
        out_specs=[pl.BlockSpec((1, tile, D_MODEL), lambda t: (ffn_tile(t) // n_s, ffn_tile(t) % n_s, 0)),
                   per_batch(WINDOW, KV_COLS), per_batch(WINDOW, KV_COLS),
                   per_batch(CONV_STATE, CONV_CH)],
        out_shape=[jax.ShapeDtypeStruct((batch, seq, D_MODEL), F32),
                   jax.ShapeDtypeStruct((batch, WINDOW, KV_COLS), F32),
                   jax.ShapeDtypeStruct((batch, WINDOW, KV_COLS), F32),
                   jax.ShapeDtypeStruct((batch, CONV_STATE, CONV_CH), F32)],
        scratch_shapes=[pltpu.VMEM((tile, Q_COLS), BF16),
                        pltpu.VMEM((WINDOW + tile, LANES), BF16),
                        pltpu.VMEM((WINDOW + tile, LANES), BF16),
                        pltpu.VMEM((SUBLANES, CONV_HALO + tile, CONV_CH), F32),
                        pltpu.VMEM((tile, D_MODEL), BF16),
                        pltpu.VMEM((tile, D_MODEL), F32),
                        pltpu.VMEM((2, tile, D_MODEL), BF16)],
        compiler_params=pltpu.CompilerParams(
            dimension_semantics=("arbitrary",),
            vmem_limit_bytes=VMEM_LIMIT_BYTES),
        name="prompt_layer",
    )(sinks, x, *tables, *weights)


def _sample_layer(layer, x, cache_k, cache_v, state, sinks, tables, weights):
    n_seq, t_new, _ = x.shape
    n_tok = n_seq * t_new
    full = lambda *shape: pl.BlockSpec(shape, lambda i: (0,) * len(shape))
    once = lambda *shape: pl.BlockSpec(shape, lambda i: (0,) * len(shape), pipeline_mode=pl.Buffered(1))
    y, sk, sv, sc = pl.pallas_call(
        functools.partial(_sample_layer_kernel, layer=layer, n_seq=n_seq, t_new=t_new),
        grid=(1,),
        in_specs=[_smem(), once(n_tok, D_MODEL), _layer_block(layer, n_seq, WINDOW, KV_COLS),
                  _layer_block(layer, n_seq, WINDOW, KV_COLS), _layer_block(layer, n_seq, CONV_STATE, CONV_CH),
                  once(n_tok, LANES), once(n_tok, LANES), once(n_tok, LANES)] + _layer_weight_specs(layer),
        out_specs=[full(n_tok, D_MODEL), full(n_seq, WINDOW, KV_COLS), full(n_seq, WINDOW, KV_COLS),
                   full(n_seq, CONV_STATE, CONV_CH)],
        out_shape=[jax.ShapeDtypeStruct((n_tok, D_MODEL), F32),
                   jax.ShapeDtypeStruct((n_seq, WINDOW, KV_COLS), F32),
                   jax.ShapeDtypeStruct((n_seq, WINDOW, KV_COLS), F32),
                   jax.ShapeDtypeStruct((n_seq, CONV_STATE, CONV_CH), F32)],
        scratch_shapes=[pltpu.VMEM((n_tok, Q_COLS), BF16),
                        pltpu.VMEM((n_seq, WINDOW + t_new, LANES), F32),
                        pltpu.VMEM((n_seq, WINDOW + t_new, LANES), F32),
                        pltpu.VMEM((n_seq, CONV_HALO + t_new, CONV_CH), F32),
                        pltpu.VMEM((n_tok, D_MODEL), BF16)],
        compiler_params=pltpu.CompilerParams(
            dimension_semantics=("arbitrary",),
            vmem_limit_bytes=VMEM_LIMIT_BYTES),
        name="sample_layer",
    )(sinks, x.reshape(n_tok, D_MODEL), cache_k, cache_v, state, *tables, *weights)
    return y.reshape(n_seq, t_new, D_MODEL), sk, sv, sc


def _rope_tables(pos):
    half = ROT_DIM // 2
    d = jnp.arange(LANES) % HEAD_DIM
    inv = jnp.power(jnp.float32(ROPE_THETA), -(d % half).astype(F32) * 2.0 / ROT_DIM)
    ang = pos.astype(F32)[:, None] * inv[None, :]
    cos, sin = jnp.cos(ang), jnp.sin(ang)
    d = d[None, :]
    return (jnp.where(d < ROT_DIM, cos, 1.0),
            jnp.where((d >= half) & (d < ROT_DIM), sin, 0.0),
            jnp.where(d < half, -sin, 0.0))


def _slab_order(a, axis):
    shape = a.shape
    heads = a.reshape(shape[:axis] + (N_KV_HEADS, GROUP, HEAD_DIM) + shape[axis + 1:])
    return jnp.swapaxes(heads, axis, axis + 1).reshape(shape)


def _pack_weights(norm1_g, w_in, q_norm_g, k_norm_g, conv_w, conv_b, conv_ln_g, conv_ln_b,
                  beta_attn, beta_conv, w_out, norm2_g, w_up, w_down):
    depth = w_in.shape[0]
    wide = lambda a: jnp.pad(a.astype(F32), ((0, 0), (0, D_MODEL - a.shape[1])))
    rows = [None] * N_VEC_ROWS
    rows[ROW_G1], rows[ROW_G2] = norm1_g.astype(F32), norm2_g.astype(F32)
    rows[ROW_BETA] = jnp.concatenate([_slab_order(beta_attn, 1), beta_conv], axis=1).astype(F32)
    rows[ROW_CONV_B], rows[ROW_LN_G], rows[ROW_LN_B] = wide(conv_b), wide(conv_ln_g), wide(conv_ln_b)
    rows[ROW_QG] = wide(jnp.concatenate([q_norm_g, q_norm_g], axis=1))
    rows[ROW_KG] = wide(jnp.concatenate([k_norm_g, k_norm_g], axis=1))
    vec = jnp.stack(rows, axis=1)
    lane = jnp.arange(LANES) // HEAD_DIM
    seg_ones = (lane[:, None] == lane[None, :]).astype(BF16)
    w_in_s = jnp.concatenate([_slab_order(w_in[:, :, :Q_COLS], 2), w_in[:, :, Q_COLS:]], axis=2)
    w_out_s = jnp.concatenate([_slab_order(w_out[:, :ATTN_WIDTH], 1), w_out[:, ATTN_WIDTH:]], axis=1)
    assert vec.shape == (depth, N_VEC_ROWS, D_MODEL)
    conv_w8 = jnp.broadcast_to(conv_w.astype(F32)[:, :, None, :], (depth, CONV_WIDTH, SUBLANES, CONV_CH))
    return [vec, w_in_s.astype(BF16), seg_ones, conv_w8, w_out_s.astype(BF16),
            w_up.astype(BF16), w_down.astype(BF16)]


def kernel(x_prompt, x_sample, cache_k, cache_v, state_conv, norm1_g, w_in, q_norm_g, k_norm_g,
           attn_sinks, conv_w, conv_b, conv_ln_g, conv_ln_b, beta_attn, beta_conv, w_out,
           norm2_g, w_up, w_down):
    batch, seq, _ = x_prompt.shape
    n_seq, t_new, _ = x_sample.shape
    tile = min(PROMPT_TILE, seq)
    assert seq % tile == 0 and tile % CHUNK == 0 and tile >= WINDOW and D_FF % (tile // CHUNK) == 0
    assert n_seq % SAMPLE_UNROLL == 0
    tables_p = _rope_tables(jnp.arange(seq))
    tables_s = tuple(jnp.tile(t, (n_seq, 1)) for t in _rope_tables(PAST_LEN + jnp.arange(t_new)))
    weights = _pack_weights(norm1_g, w_in, q_norm_g, k_norm_g, conv_w, conv_b, conv_ln_g, conv_ln_b,
                            beta_attn, beta_conv, w_out, norm2_g, w_up, w_down)
    sinks = attn_sinks.astype(F32)
    cache_k = cache_k.reshape(DEPTH, n_seq, WINDOW, KV_COLS)
    cache_v = cache_v.reshape(DEPTH, n_seq, WINDOW, KV_COLS)
    yp, ys = x_prompt, x_sample
    outs = [[] for _ in range(6)]
    for l in range(DEPTH):
        yp, pk, pv, pc = _prompt_layer(l, yp, sinks, tables_p, weights, tile)
        ys, sk, sv, sc = _sample_layer(l, ys, cache_k, cache_v, state_conv, sinks, tables_s, weights)
        kv_shape = lambda a: a.reshape(a.shape[0], WINDOW, N_KV_HEADS, HEAD_DIM)
        for dst, val in zip(outs, (kv_shape(pk), kv_shape(pv), pc, kv_shape(sk), kv_shape(sv), sc)):
            dst.append(val)
    return (yp, ys) + tuple(jnp.stack(o) for o in outs)
```

```python
import functools

import jax
import jax.numpy as jnp
from jax import lax
from jax.experimental import pallas as pl
from jax.experimental.pallas import tpu as pltpu

D_MODEL = 1024
DEPTH = 2
PAST_LEN = 2048
CHUNK = 64
WINDOW = 128
HEAD_DIM = 64
ATTN_WIDTH = D_MODEL // 2
N_HEADS = ATTN_WIDTH // HEAD_DIM
N_KV_HEADS = max(1, N_HEADS // 4)
GROUP = N_HEADS // N_KV_HEADS
CONV_CH = D_MODEL - ATTN_WIDTH
CONV_WIDTH = 31
CONV_STATE = CONV_WIDTH - 1
ROT_DIM = HEAD_DIM // 4
ROPE_THETA = 500000.0
D_FF = 4 * D_MODEL
EPS = 1e-6
Q_COLS = N_HEADS * HEAD_DIM
KV_COLS = N_KV_HEADS * HEAD_DIM
IN_COLS = Q_COLS + 2 * KV_COLS + 2 * CONV_CH

LANES = 128
SUBLANES = 8
N_SLABS = Q_COLS // LANES
CONV_HALO = 32
CONV_SHIFT = CONV_HALO - CONV_STATE
PROMPT_TILE = 512
CONV_ROWS = 32
SAMPLE_UNROLL = 8
SAMPLE_FFN_PIECES = 4
VMEM_LIMIT_BYTES = 56 * 1024 * 1024

ROW_G1, ROW_G2, ROW_BETA, ROW_CONV_B, ROW_LN_G, ROW_LN_B, ROW_QG, ROW_KG = range(8)
N_VEC_ROWS = 8

LOG2_E = 1.4426950408889634
Q_SCALE = HEAD_DIM ** -0.5 * LOG2_E

F32 = jnp.float32
BF16 = jnp.bfloat16

assert N_KV_HEADS == 2 and KV_COLS == LANES and GROUP == N_SLABS


def _dot(a, b):
    return jnp.dot(a, b, preferred_element_type=F32)


def _dot_nt(a, b):
    return lax.dot_general(a, b, (((1,), (1,)), ((), ())), preferred_element_type=F32)


def _rms_norm(x, g):
    ms = jnp.mean(x * x, axis=-1, keepdims=True)
    return x * lax.rsqrt(ms + EPS) * g


def _lane_lo(shape):
    return lax.broadcasted_iota(jnp.int32, shape, len(shape) - 1) % LANES < HEAD_DIM


def _head_norm_rope(z, gain, seg_ones, cos, sin_prev, sin_next):
    out = []
    for i in range(z.shape[1] // LANES):
        zi = z[:, i * LANES:(i + 1) * LANES]
        ssq = _dot((zi * zi).astype(BF16), seg_ones)
        zn = zi * lax.rsqrt(ssq * (1.0 / HEAD_DIM) + EPS) * gain
        prev = pltpu.roll(zn, ROT_DIM // 2, axis=1)
        nxt = pltpu.roll(zn, LANES - ROT_DIM // 2, axis=1)
        out.append(zn * cos + prev * sin_prev + nxt * sin_next)
    return out[0] if len(out) == 1 else jnp.concatenate(out, axis=1)


def _sink_column(sinks_ref, layer, rows):
    n_blocks = 2 * N_SLABS
    blk = lax.broadcasted_iota(jnp.int32, (n_blocks * rows, 1), 0) // rows
    col = jnp.zeros((n_blocks * rows, 1), F32)
    for i in range(N_SLABS):
        for half in range(2):
            col = jnp.where(blk == 2 * i + half, sinks_ref[layer, i + GROUP * half] * LOG2_E, col)
    return col


def _attention(q, k, v, sink_col, valid):
    rows = q.shape[0]
    lo = _lane_lo((rows, LANES))
    zero = jnp.zeros((rows, LANES), q.dtype)
    parts = []
    for i in range(N_SLABS):
        slab = q[:, i * LANES:(i + 1) * LANES]
        parts += [jnp.where(lo, slab, zero), jnp.where(lo, zero, slab)]
    s = _dot_nt(jnp.concatenate(parts, axis=0), k)
    if valid is not None:
        s = jnp.where(valid, s, -jnp.inf)
    m = jnp.maximum(jnp.max(s, axis=-1, keepdims=True), sink_col)
    p = jnp.exp2(s - m)
    denom = jnp.sum(p, axis=-1, keepdims=True) + jnp.exp2(sink_col - m)
    o = _dot(p.astype(BF16), v) * (1.0 / denom)
    return jnp.concatenate(
        [jnp.where(lo, o[2 * i * rows:(2 * i + 1) * rows], o[(2 * i + 1) * rows:(2 * i + 2) * rows])
         for i in range(N_SLABS)], axis=1)


def _conv_ln_silu(u_window, cw_ref, cb, ln_g, ln_b):
    def tap(j):
        u = u_window(j)
        return u.reshape(u.shape[0] // SUBLANES, SUBLANES, u.shape[1]) * cw_ref[j][None]
    acc = tap(0)
    for j in range(1, CONV_WIDTH):
        acc = acc + tap(j)
    y = acc.reshape(acc.shape[0] * SUBLANES, acc.shape[2]) + cb
    mu = jnp.mean(y, axis=-1, keepdims=True)
    yc = y - mu
    var = jnp.mean(yc * yc, axis=-1, keepdims=True)
    yn = yc * lax.rsqrt(var + EPS) * ln_g + ln_b
    return yn * jax.nn.sigmoid(yn)


def _ffn_part(h2, wup_ref, wdn_ref, c0, width):
    up = jnp.maximum(_dot(h2, wup_ref[:, c0:c0 + width]).astype(BF16), 0.0)
    return _dot(up * up, wdn_ref[c0:c0 + width, :])


def _vec(vec_ref, row, width=D_MODEL):
    return vec_ref[row:row + 1, 0:width]


def _prompt_layer_kernel(sinks_ref, x_ref, cos_ref, sp_ref, sn_ref, vec_ref, win_ref, seg_ref, cw_ref,
                         wout_ref, wup_ref, wdn_ref,
                         y_ref, pk_ref, pv_ref, pc_ref,
                         q_scr, k_scr, v_scr, u_scr, cat_scr, x1_scr, h2_scr,
                         *, layer, tile, n_s, n_tiles):
    t = pl.program_id(0)
    s_idx = jnp.minimum(t, n_tiles - 1) % n_s
    slot = t % 2
    n_pieces = tile // CHUNK
    ffn_width = D_FF // n_pieces

    @pl.when(t == 0)
    def _():
        x1_scr[...] = jnp.zeros(x1_scr.shape, F32)
        h2_scr[...] = jnp.zeros(h2_scr.shape, BF16)

    @pl.when(s_idx == 0)
    def _():
        k_scr[0:WINDOW, :] = jnp.zeros((WINDOW, LANES), BF16)
        v_scr[0:WINDOW, :] = jnp.zeros((WINDOW, LANES), BF16)
        u_scr[0, 0:CONV_HALO, :] = jnp.zeros((CONV_HALO, CONV_CH), F32)

    def ffn_piece(c):
        part = _ffn_part(h2_scr[slot], wup_ref, wdn_ref, c * ffn_width, ffn_width)
        if c == 0:
            y_ref[0] = x1_scr[...] + part
        else:
            y_ref[0] += part

    ffn_piece(0)

    h = _rms_norm(x_ref[0], _vec(vec_ref, ROW_G1)).astype(BF16)
    z = _dot(h, win_ref[...])
    cos, sp, sn = cos_ref[...], sp_ref[...], sn_ref[...]
    seg = seg_ref[...]
    q = _head_norm_rope(z[:, :Q_COLS], _vec(vec_ref, ROW_QG, LANES), seg, cos, sp, sn) * Q_SCALE
    q_scr[...] = q.astype(BF16)
    k = _head_norm_rope(z[:, Q_COLS:Q_COLS + KV_COLS], _vec(vec_ref, ROW_KG, LANES), seg, cos, sp, sn)
    v = z[:, Q_COLS + KV_COLS:Q_COLS + 2 * KV_COLS]
    k_scr[WINDOW:WINDOW + tile, :] = k.astype(BF16)
    v_scr[WINDOW:WINDOW + tile, :] = v.astype(BF16)
    pk_ref[0] = k[tile - WINDOW:tile, :]
    pv_ref[0] = v[tile - WINDOW:tile, :]
    a_off = Q_COLS + 2 * KV_COLS
    u = z[:, a_off:a_off + CONV_CH] * jax.nn.sigmoid(z[:, a_off + CONV_CH:a_off + 2 * CONV_CH])
    u_scr[0, CONV_HALO:CONV_HALO + tile, :] = u
    pc_ref[0] = u_scr[0, CONV_HALO + tile - CONV_STATE:CONV_HALO + tile, :]
    n_shift_rows = CONV_HALO + tile - SUBLANES
    n_groups = (CONV_HALO + tile) // SUBLANES
    u_groups = u_scr[0].reshape(n_groups, SUBLANES, CONV_CH)
    row_in_group = lax.broadcasted_iota(jnp.int32, (n_groups - 1, SUBLANES, CONV_CH), 1)
    for b in range(1, SUBLANES):
        rot = pltpu.roll(u_groups, SUBLANES - b, axis=1)
        shifted = jnp.where(row_in_group < SUBLANES - b, rot[:-1], rot[1:])
        u_scr[b, 0:n_shift_rows, :] = shifted.reshape(n_shift_rows, CONV_CH)

    beta = _vec(vec_ref, ROW_BETA)
    cb, ln_g, ln_b = (_vec(vec_ref, r, CONV_CH) for r in (ROW_CONV_B, ROW_LN_G, ROW_LN_B))
    sink_col = _sink_column(sinks_ref, layer, CHUNK)
    n_keys = WINDOW + CHUNK
    key_j = lax.broadcasted_iota(jnp.int32, (1, n_keys), 1)

    def mixer_piece(c):
        r0 = c * CHUNK
        valid = None
        if r0 < WINDOW:
            valid = jnp.logical_or(key_j >= WINDOW - r0, s_idx > 0)
        a_o = _attention(q_scr[r0:r0 + CHUNK, :], k_scr[r0:r0 + n_keys, :], v_scr[r0:r0 + n_keys, :],
                         sink_col, valid)
        cat_scr[r0:r0 + CHUNK, 0:ATTN_WIDTH] = (a_o * beta[:, 0:ATTN_WIDTH]).astype(BF16)

        for r in range(r0, r0 + CHUNK, CONV_ROWS):
            def window(j, r=r):
                off = CONV_SHIFT + j
                b = off % SUBLANES
                return u_scr[b, r + off - b:r + off - b + CONV_ROWS, :]
            c_o = _conv_ln_silu(window, cw_ref, cb, ln_g, ln_b)
            cat_scr[r:r + CONV_ROWS, ATTN_WIDTH:] = (c_o * beta[:, ATTN_WIDTH:]).astype(BF16)

    for c in range(n_pieces):
        if c + 1 < n_pieces - 1:
            ffn_piece(c + 1)
        mixer_piece(c)

    k_scr[0:WINDOW, :] = k_scr[tile:tile + WINDOW, :]
    v_scr[0:WINDOW, :] = v_scr[tile:tile + WINDOW, :]
    u_scr[0, 0:CONV_HALO, :] = u_scr[0, tile:tile + CONV_HALO, :]
    x1 = x_ref[0] + _dot(cat_scr[...], wout_ref[...])
    ffn_piece(n_pieces - 1)
    x1_scr[...] = x1
    h2_scr[1 - slot] = _rms_norm(x1, _vec(vec_ref, ROW_G2)).astype(BF16)


def _sample_layer_kernel(sinks_ref, x_ref, ck_ref, cv_ref, st_ref, cos_ref, sp_ref, sn_ref, vec_ref,
                         win_ref, seg_ref, cw_ref, wout_ref, wup_ref, wdn_ref,
                         y_ref, sk_ref, sv_ref, sc_ref,
                         q_scr, kx_scr, vx_scr, ux_scr, cat_scr, *, layer, n_seq, t_new):
    x = x_ref[...]
    h = _rms_norm(x, _vec(vec_ref, ROW_G1)).astype(BF16)
    z = _dot(h, win_ref[...])

    cos, sp, sn = cos_ref[...], sp_ref[...], sn_ref[...]
    seg = seg_ref[...]
    q = _head_norm_rope(z[:, :Q_COLS], _vec(vec_ref, ROW_QG, LANES), seg, cos, sp, sn) * Q_SCALE
    q_scr[...] = q.astype(BF16)
    k = _head_norm_rope(z[:, Q_COLS:Q_COLS + KV_COLS], _vec(vec_ref, ROW_KG, LANES), seg, cos, sp, sn)
    v = z[:, Q_COLS + KV_COLS:Q_COLS + 2 * KV_COLS]
    a_off = Q_COLS + 2 * KV_COLS
    u = z[:, a_off:a_off + CONV_CH] * jax.nn.sigmoid(z[:, a_off + CONV_CH:a_off + 2 * CONV_CH])

    kx_scr[:, 0:WINDOW, :] = ck_ref[...]
    vx_scr[:, 0:WINDOW, :] = cv_ref[...]
    kx_scr[:, WINDOW:WINDOW + t_new, :] = k.reshape(n_seq, t_new, LANES)
    vx_scr[:, WINDOW:WINDOW + t_new, :] = v.reshape(n_seq, t_new, LANES)
    ux_scr[:, CONV_SHIFT:CONV_HALO, :] = st_ref[...]
    ux_scr[:, CONV_HALO:CONV_HALO + t_new, :] = u.reshape(n_seq, t_new, CONV_CH)
    sk_ref[...] = kx_scr[:, t_new:t_new + WINDOW, :]
    sv_ref[...] = vx_scr[:, t_new:t_new + WINDOW, :]
    sc_ref[...] = ux_scr[:, CONV_HALO + t_new - CONV_STATE:CONV_HALO + t_new, :]

    beta = _vec(vec_ref, ROW_BETA)
    cb, ln_g, ln_b = (_vec(vec_ref, r, CONV_CH) for r in (ROW_CONV_B, ROW_LN_G, ROW_LN_B))
    sink_col = _sink_column(sinks_ref, layer, t_new)

    def per_group(i, carry):
        for j in range(SAMPLE_UNROLL):
            b = i * SAMPLE_UNROLL + j
            r0 = pl.multiple_of(b * t_new, t_new)
            a_o = _attention(q_scr[pl.ds(r0, t_new), :], kx_scr[b].astype(BF16), vx_scr[b].astype(BF16),
                             sink_col, None)
            cat_scr[pl.ds(r0, t_new), 0:ATTN_WIDTH] = (a_o * beta[:, 0:ATTN_WIDTH]).astype(BF16)
            window = lambda tap, b=b: ux_scr[b, CONV_SHIFT + tap:CONV_SHIFT + tap + t_new, :]
            c_o = _conv_ln_silu(window, cw_ref, cb, ln_g, ln_b)
            cat_scr[pl.ds(r0, t_new), ATTN_WIDTH:] = (c_o * beta[:, ATTN_WIDTH:]).astype(BF16)
        return carry

    lax.fori_loop(0, n_seq // SAMPLE_UNROLL, per_group, 0)

    x1 = x + _dot(cat_scr[...], wout_ref[...])
    h2 = _rms_norm(x1, _vec(vec_ref, ROW_G2)).astype(BF16)
    y = x1
    for c in range(SAMPLE_FFN_PIECES):
        y = y + _ffn_part(h2, wup_ref, wdn_ref, c * (D_FF // SAMPLE_FFN_PIECES), D_FF // SAMPLE_FFN_PIECES)
    y_ref[...] = y


def _layer_block(layer, *shape):
    return pl.BlockSpec((None,) + shape, lambda *_: (layer,) + (0,) * len(shape),
                        pipeline_mode=pl.Buffered(1))


def _smem():
    return pl.BlockSpec(memory_space=pltpu.SMEM)


def _layer_weight_specs(layer):
    return [
        _layer_block(layer, N_VEC_ROWS, D_MODEL),
        _layer_block(layer, D_MODEL, IN_COLS),
        pl.BlockSpec((LANES, LANES), lambda *_: (0, 0), pipeline_mode=pl.Buffered(1)),
        _layer_block(layer, CONV_WIDTH, SUBLANES, CONV_CH),
        _layer_block(layer, D_MODEL, D_MODEL),
        _layer_block(layer, D_MODEL, D_FF),
        _layer_block(layer, D_FF, D_MODEL),
    ]


def _prompt_layer(layer, x, sinks, tables, weights, tile):
    batch, seq, _ = x.shape
    n_s = seq // tile
    n_tiles = batch * n_s
    mixer_tile = lambda t: jnp.minimum(t, n_tiles - 1)
    ffn_tile = lambda t: jnp.maximum(t - 1, 0)
    table_spec = pl.BlockSpec((tile, LANES), lambda t: (mixer_tile(t) % n_s, 0))
    per_batch = lambda rows, cols: pl.BlockSpec((1, rows, cols), lambda t: (mixer_tile(t) // n_s, 0, 0))
    return pl.pallas_call(
        functools.partial(_prompt_layer_kernel, layer=layer, tile=tile, n_s=n_s, n_tiles=n_tiles),
        grid=(n_tiles + 1,),
        in_specs=[_smem(),
                  pl.BlockSpec((1, tile, D_MODEL), lambda t: (mixer_tile(t) // n_s, mixer_tile(t) % n_s, 0)),
                  table_spec, table_spec, table_spec] + _layer_weight_specs(layer),
        out_specs=[pl.BlockSpec((1, tile, D_MODEL), lambda t: (ffn_tile(t) // n_s, ffn_tile(t) % n_s, 0)),
                   per_batch(WINDOW, KV_COLS), per_batch(WINDOW, KV_COLS),
                   per_batch(CONV_STATE, CONV_CH)],
        out_shape=[jax.ShapeDtypeStruct((batch, seq, D_MODEL), F32),
                   jax.ShapeDtypeStruct((batch, WINDOW, KV_COLS), F32),
                   jax.ShapeDtypeStruct((batch, WINDOW, KV_COLS), F32),
                   jax.ShapeDtypeStruct((batch, CONV_STATE, CONV_CH), F32)],
        scratch_shapes=[pltpu.VMEM((tile, Q_COLS), BF16),
                        pltpu.VMEM((WINDOW + tile, LANES), BF16),
                        pltpu.VMEM((WINDOW + tile, LANES), BF16),
                        pltpu.VMEM((SUBLANES, CONV_HALO + tile, CONV_CH), F32),
                        pltpu.VMEM((tile, D_MODEL), BF16),
                        pltpu.VMEM((tile, D_MODEL), F32),
                        pltpu.VMEM((2, tile, D_MODEL), BF16)],
        compiler_params=pltpu.CompilerParams(
            dimension_semantics=("arbitrary",),
            vmem_limit_bytes=VMEM_LIMIT_BYTES),
        name="prompt_layer",
    )(sinks, x, *tables, *weights)


def _sample_layer(layer, x, cache_k, cache_v, state, sinks, tables, weights):
    n_seq, t_new, _ = x.shape
    n_tok = n_seq * t_new
    full = lambda *shape: pl.BlockSpec(shape, lambda i: (0,) * len(shape))
    once = lambda *shape: pl.BlockSpec(shape, lambda i: (0,) * len(shape), pipeline_mode=pl.Buffered(1))
    y, sk, sv, sc = pl.pallas_call(
        functools.partial(_sample_layer_kernel, layer=layer, n_seq=n_seq, t_new=t_new),
        grid=(1,),
        in_specs=[_smem(), once(n_tok, D_MODEL), _layer_block(layer, n_seq, WINDOW, KV_COLS),
                  _layer_block(layer, n_seq, WINDOW, KV_COLS), _layer_block(layer, n_seq, CONV_STATE, CONV_CH),
                  once(n_tok, LANES), once(n_tok, LANES), once(n_tok, LANES)] + _layer_weight_specs(layer),
        out_specs=[full(n_tok, D_MODEL), full(n_seq, WINDOW, KV_COLS), full(n_seq, WINDOW, KV_COLS),
                   full(n_seq, CONV_STATE, CONV_CH)],
        out_shape=[jax.ShapeDtypeStruct((n_tok, D_MODEL), F32),
                   jax.ShapeDtypeStruct((n_seq, WINDOW, KV_COLS), F32),
                   jax.ShapeDtypeStruct((n_seq, WINDOW, KV_COLS), F32),
                   jax.ShapeDtypeStruct((n_seq, CONV_STATE, CONV_CH), F32)],
        scratch_shapes=[pltpu.VMEM((n_tok, Q_COLS), BF16),
                        pltpu.VMEM((n_seq, WINDOW + t_new, LANES), F32),
                        pltpu.VMEM((n_seq, WINDOW + t_new, LANES), F32),
                        pltpu.VMEM((n_seq, CONV_HALO + t_new, CONV_CH), F32),
                        pltpu.VMEM((n_tok, D_MODEL), BF16)],
        compiler_params=pltpu.CompilerParams(
            dimension_semantics=("arbitrary",),
            vmem_limit_bytes=VMEM_LIMIT_BYTES),
        name="sample_layer",
    )(sinks, x.reshape(n_tok, D_MODEL), cache_k, cache_v, state, *tables, *weights)
    return y.reshape(n_seq, t_new, D_MODEL), sk, sv, sc


def _rope_tables(pos):
    half = ROT_DIM // 2
    d = jnp.arange(LANES) % HEAD_DIM
    inv = jnp.power(jnp.float32(ROPE_THETA), -(d % half).astype(F32) * 2.0 / ROT_DIM)
    ang = pos.astype(F32)[:, None] * inv[None, :]
    cos, sin = jnp.cos(ang), jnp.sin(ang)
    d = d[None, :]
    return (jnp.where(d < ROT_DIM, cos, 1.0),
            jnp.where((d >= half) & (d < ROT_DIM), sin, 0.0),
            jnp.where(d < half, -sin, 0.0))


def _slab_order(a, axis):
    shape = a.shape
    heads = a.reshape(shape[:axis] + (N_KV_HEADS, GROUP, HEAD_DIM) + shape[axis + 1:])
    return jnp.swapaxes(heads, axis, axis + 1).reshape(shape)


def _pack_weights(norm1_g, w_in, q_norm_g, k_norm_g, conv_w, conv_b, conv_ln_g, conv_ln_b,
                  beta_attn, beta_conv, w_out, norm2_g, w_up, w_down):
    depth = w_in.shape[0]
    wide = lambda a: jnp.pad(a.astype(F32), ((0, 0), (0, D_MODEL - a.shape[1])))
    rows = [None] * N_VEC_ROWS
    rows[ROW_G1], rows[ROW_G2] = norm1_g.astype(F32), norm2_g.astype(F32)
    rows[ROW_BETA] = jnp.concatenate([_slab_order(beta_attn, 1), beta_conv], axis=1).astype(F32)
    rows[ROW_CONV_B], rows[ROW_LN_G], rows[ROW_LN_B] = wide(conv_b), wide(conv_ln_g), wide(conv_ln_b)
    rows[ROW_QG] = wide(jnp.concatenate([q_norm_g, q_norm_g], axis=1))
    rows[ROW_KG] = wide(jnp.concatenate([k_norm_g, k_norm_g], axis=1))
    vec = jnp.stack(rows, axis=1)
    lane = jnp.arange(LANES) // HEAD_DIM
    seg_ones = (lane[:, None] == lane[None, :]).astype(BF16)
    w_in_s = jnp.concatenate([_slab_order(w_in[:, :, :Q_COLS], 2), w_in[:, :, Q_COLS:]], axis=2)
    w_out_s = jnp.concatenate([_slab_order(w_out[:, :ATTN_WIDTH], 1), w_out[:, ATTN_WIDTH:]], axis=1)
    assert vec.shape == (depth, N_VEC_ROWS, D_MODEL)
    conv_w8 = jnp.broadcast_to(conv_w.astype(F32)[:, :, None, :], (depth, CONV_WIDTH, SUBLANES, CONV_CH))
    return [vec, w_in_s.astype(BF16), seg_ones, conv_w8, w_out_s.astype(BF16),
            w_up.astype(BF16), w_down.astype(BF16)]


def kernel(x_prompt, x_sample, cache_k, cache_v, state_conv, norm1_g, w_in, q_norm_g, k_norm_g,
           attn_sinks, conv_w, conv_b, conv_ln_g, conv_ln_b, beta_attn, beta_conv, w_out,
           norm2_g, w_up, w_down):
    batch, seq, _ = x_prompt.shape
    n_seq, t_new, _ = x_sample.shape
    tile = min(PROMPT_TILE, seq)
    assert seq % tile == 0 and tile % CHUNK == 0 and tile >= WINDOW and D_FF % (tile // CHUNK) == 0
    assert n_seq % SAMPLE_UNROLL == 0
    tables_p = _rope_tables(jnp.arange(seq))
    tables_s = tuple(jnp.tile(t, (n_seq, 1)) for t in _rope_tables(PAST_LEN + jnp.arange(t_new)))
    weights = _pack_weights(norm1_g, w_in, q_norm_g, k_norm_g, conv_w, conv_b, conv_ln_g, conv_ln_b,
                            beta_attn, beta_conv, w_out, norm2_g, w_up, w_down)
    sinks = attn_sinks.astype(F32)
    cache_k = cache_k.reshape(DEPTH, n_seq, WINDOW, KV_COLS)
    cache_v = cache_v.reshape(DEPTH, n_seq, WINDOW, KV_COLS)
    yp, ys = x_prompt, x_sample
    outs = [[] for _ in range(6)]
    for l in range(DEPTH):
        yp, pk, pv, pc = _prompt_layer(l, yp, sinks, tables_p, weights, tile)
        ys, sk, sv, sc = _sample_layer(l, ys, cache_k, cache_v, state_conv, sinks, tables_s, weights)
        kv_shape = lambda a: a.reshape(a.shape[0], WINDOW, N_KV_HEADS, HEAD_DIM)
        for dst, val in zip(outs, (kv_shape(pk), kv_shape(pv), pc, kv_shape(sk), kv_shape(sv), sc)):
            dst.append(val)
    return (yp, ys) + tuple(jnp.stack(o) for o in outs)
```

```python
import functools

import jax
import numpy as np
import jax.numpy as jnp
from jax import lax
from jax.experimental import pallas as pl
from jax.experimental.pallas import tpu as pltpu

D_MODEL = 1024
DEPTH = 2
PAST_LEN = 2048
CHUNK = 64
WINDOW = 128
HEAD_DIM = 64
ATTN_WIDTH = D_MODEL // 2
N_HEADS = ATTN_WIDTH // HEAD_DIM
N_KV_HEADS = max(1, N_HEADS // 4)
GROUP = N_HEADS // N_KV_HEADS
CONV_CH = D_MODEL - ATTN_WIDTH
CONV_WIDTH = 31
CONV_STATE = CONV_WIDTH - 1
ROT_DIM = HEAD_DIM // 4
ROPE_THETA = 500000.0
D_FF = 4 * D_MODEL
EPS = 1e-6
Q_COLS = N_HEADS * HEAD_DIM
KV_COLS = N_KV_HEADS * HEAD_DIM
IN_COLS = Q_COLS + 2 * KV_COLS + 2 * CONV_CH

LANES = 128
SUBLANES = 8
N_SLABS = Q_COLS // LANES
CONV_HALO = 32
CONV_SHIFT = CONV_HALO - CONV_STATE
PROMPT_TILE = 512
CONV_ROWS = 32
SAMPLE_UNROLL = 8
SAMPLE_FFN_PIECES = 4
VMEM_LIMIT_BYTES = 56 * 1024 * 1024

ROW_G1, ROW_G2, ROW_BETA, ROW_CONV_B, ROW_LN_G, ROW_LN_B, ROW_QG, ROW_KG = range(8)
N_VEC_ROWS = 8

LOG2_E = 1.4426950408889634
Q_SCALE = HEAD_DIM ** -0.5 * LOG2_E

F32 = jnp.float32
BF16 = jnp.bfloat16

assert N_KV_HEADS == 2 and KV_COLS == LANES and GROUP == N_SLABS


def _dot(a, b):
    return jnp.dot(a, b, preferred_element_type=F32)


def _dot_nt(a, b):
    return lax.dot_general(a, b, (((1,), (1,)), ((), ())), preferred_element_type=F32)


def _rms_norm(x, g):
    ms = jnp.mean(x * x, axis=-1, keepdims=True)
    return x * lax.rsqrt(ms + EPS) * g


def _lane_lo(shape):
    return lax.broadcasted_iota(jnp.int32, shape, len(shape) - 1) % LANES < HEAD_DIM


def _head_norm_rope(z, gain, seg_ones, cos, sin_prev, sin_next):
    out = []
    for i in range(z.shape[1] // LANES):
        zi = z[:, i * LANES:(i + 1) * LANES]
        ssq = _dot((zi * zi).astype(BF16), seg_ones)
        zn = zi * lax.rsqrt(ssq * (1.0 / HEAD_DIM) + EPS) * gain
        prev = pltpu.roll(zn, ROT_DIM // 2, axis=1)
        nxt = pltpu.roll(zn, LANES - ROT_DIM // 2, axis=1)
        out.append(zn * cos + prev * sin_prev + nxt * sin_next)
    return out[0] if len(out) == 1 else jnp.concatenate(out, axis=1)


def _sink_column(sinks_ref, layer, rows):
    n_blocks = 2 * N_SLABS
    blk = lax.broadcasted_iota(jnp.int32, (n_blocks * rows, 1), 0) // rows
    col = jnp.zeros((n_blocks * rows, 1), F32)
    for i in range(N_SLABS):
        for half in range(2):
            col = jnp.where(blk == 2 * i + half, sinks_ref[layer, i + GROUP * half] * LOG2_E, col)
    return col


def _attention(q, k, v, sink_col, valid):
    rows = q.shape[0]
    lo = _lane_lo((rows, LANES))
    zero = jnp.zeros((rows, LANES), q.dtype)
    parts = []
    for i in range(N_SLABS):
        slab = q[:, i * LANES:(i + 1) * LANES]
        parts += [jnp.where(lo, slab, zero), jnp.where(lo, zero, slab)]
    s = _dot_nt(jnp.concatenate(parts, axis=0), k)
    if valid is not None:
        s = jnp.where(valid, s, -jnp.inf)
    m = jnp.maximum(jnp.max(s, axis=-1, keepdims=True), sink_col)
    p = jnp.exp2(s - m)
    denom = jnp.sum(p, axis=-1, keepdims=True) + jnp.exp2(sink_col - m)
    o = _dot(p.astype(BF16), v) * (1.0 / denom)
    return jnp.concatenate(
        [jnp.where(lo, o[2 * i * rows:(2 * i + 1) * rows], o[(2 * i + 1) * rows:(2 * i + 2) * rows])
         for i in range(N_SLABS)], axis=1)


def _conv_ln_silu(u_window, cw_ref, cb, ln_g, ln_b):
    def tap(j):
        u = u_window(j)
        return u.reshape(u.shape[0] // SUBLANES, SUBLANES, u.shape[1]) * cw_ref[j][None]
    acc = tap(0)
    for j in range(1, CONV_WIDTH):
        acc = acc + tap(j)
    y = acc.reshape(acc.shape[0] * SUBLANES, acc.shape[2]) + cb
    mu = jnp.mean(y, axis=-1, keepdims=True)
    yc = y - mu
    var = jnp.mean(yc * yc, axis=-1, keepdims=True)
    yn = yc * lax.rsqrt(var + EPS) * ln_g + ln_b
    return yn * jax.nn.sigmoid(yn)


def _ffn_part(h2, wup_ref, wdn_ref, c0, width):
    up = jnp.maximum(_dot(h2, wup_ref[:, c0:c0 + width]).astype(BF16), 0.0)
    return _dot(up * up, wdn_ref[c0:c0 + width, :])


def _vec(vec_ref, row, width=D_MODEL):
    return vec_ref[row:row + 1, 0:width]


def _prompt_layer_kernel(sinks_ref, x_ref, cos_ref, sp_ref, sn_ref, vec_ref, win_ref, seg_ref, cw_ref,
                         wout_ref, wup_ref, wdn_ref,
                         y_ref, pk_ref, pv_ref, pc_ref,
                         q_scr, k_scr, v_scr, u_scr, cat_scr, x1_scr, h2_scr,
                         *, layer, tile, n_s, n_tiles):
    t = pl.program_id(0)
    s_idx = jnp.minimum(t, n_tiles - 1) % n_s
    slot = t % 2
    n_pieces = tile // CHUNK
    ffn_width = D_FF // n_pieces

    @pl.when(t == 0)
    def _():
        x1_scr[...] = jnp.zeros(x1_scr.shape, F32)
        h2_scr[...] = jnp.zeros(h2_scr.shape, BF16)

    @pl.when(s_idx == 0)
    def _():
        k_scr[0:WINDOW, :] = jnp.zeros((WINDOW, LANES), BF16)
        v_scr[0:WINDOW, :] = jnp.zeros((WINDOW, LANES), BF16)
        u_scr[0, 0:CONV_HALO, :] = jnp.zeros((CONV_HALO, CONV_CH), F32)

    def ffn_piece(c):
        part = _ffn_part(h2_scr[slot], wup_ref, wdn_ref, c * ffn_width, ffn_width)
        if c == 0:
            y_ref[0] = x1_scr[...] + part
        else:
            y_ref[0] += part

    ffn_piece(0)

    h = _rms_norm(x_ref[0], _vec(vec_ref, ROW_G1)).astype(BF16)
    z = _dot(h, win_ref[...])
    cos, sp, sn = cos_ref[...], sp_ref[...], sn_ref[...]
    seg = seg_ref[...]
    q = _head_norm_rope(z[:, :Q_COLS], _vec(vec_ref, ROW_QG, LANES), seg, cos, sp, sn) * Q_SCALE
    q_scr[...] = q.astype(BF16)
    k = _head_norm_rope(z[:, Q_COLS:Q_COLS + KV_COLS], _vec(vec_ref, ROW_KG, LANES), seg, cos, sp, sn)
    v = z[:, Q_COLS + KV_COLS:Q_COLS + 2 * KV_COLS]
    k_scr[WINDOW:WINDOW + tile, :] = k.astype(BF16)
    v_scr[WINDOW:WINDOW + tile, :] = v.astype(BF16)
    pk_ref[0] = k[tile - WINDOW:tile, :]
    pv_ref[0] = v[tile - WINDOW:tile, :]
    a_off = Q_COLS + 2 * KV_COLS
    u = z[:, a_off:a_off + CONV_CH] * jax.nn.sigmoid(z[:, a_off + CONV_CH:a_off + 2 * CONV_CH])
    u_scr[0, CONV_HALO:CONV_HALO + tile, :] = u
    pc_ref[0] = u_scr[0, CONV_HALO + tile - CONV_STATE:CONV_HALO + tile, :]
    n_shift_rows = CONV_HALO + tile - SUBLANES
    n_groups = (CONV_HALO + tile) // SUBLANES
    u_groups = u_scr[0].reshape(n_groups, SUBLANES, CONV_CH)
    row_in_group = lax.broadcasted_iota(jnp.int32, (n_groups - 1, SUBLANES, CONV_CH), 1)
    for b in range(1, SUBLANES):
        rot = pltpu.roll(u_groups, SUBLANES - b, axis=1)
        shifted = jnp.where(row_in_group < SUBLANES - b, rot[:-1], rot[1:])
        u_scr[b, 0:n_shift_rows, :] = shifted.reshape(n_shift_rows, CONV_CH)

    beta = _vec(vec_ref, ROW_BETA)
    cb, ln_g, ln_b = (_vec(vec_ref, r, CONV_CH) for r in (ROW_CONV_B, ROW_LN_G, ROW_LN_B))
    sink_col = _sink_column(sinks_ref, layer, CHUNK)
    n_keys = WINDOW + CHUNK
    key_j = lax.broadcasted_iota(jnp.int32, (1, n_keys), 1)

    def mixer_piece(c):
        r0 = c * CHUNK
        valid = None
        if r0 < WINDOW:
            valid = jnp.logical_or(key_j >= WINDOW - r0, s_idx > 0)
        a_o = _attention(q_scr[r0:r0 + CHUNK, :], k_scr[r0:r0 + n_keys, :], v_scr[r0:r0 + n_keys, :],
                         sink_col, valid)
        cat_scr[r0:r0 + CHUNK, 0:ATTN_WIDTH] = (a_o * beta[:, 0:ATTN_WIDTH]).astype(BF16)

        for r in range(r0, r0 + CHUNK, CONV_ROWS):
            def window(j, r=r):
                off = CONV_SHIFT + j
                b = off % SUBLANES
                return u_scr[b, r + off - b:r + off - b + CONV_ROWS, :]
            c_o = _conv_ln_silu(window, cw_ref, cb, ln_g, ln_b)
            cat_scr[r:r + CONV_ROWS, ATTN_WIDTH:] = (c_o * beta[:, ATTN_WIDTH:]).astype(BF16)

    for c in range(n_pieces):
        if c + 1 < n_pieces - 1:
            ffn_piece(c + 1)
        mixer_piece(c)

    k_scr[0:WINDOW, :] = k_scr[tile:tile + WINDOW, :]
    v_scr[0:WINDOW, :] = v_scr[tile:tile + WINDOW, :]
    u_scr[0, 0:CONV_HALO, :] = u_scr[0, tile:tile + CONV_HALO, :]
    x1 = x_ref[0] + _dot(cat_scr[...], wout_ref[...])
    ffn_piece(n_pieces - 1)
    x1_scr[...] = x1
    h2_scr[1 - slot] = _rms_norm(x1, _vec(vec_ref, ROW_G2)).astype(BF16)


def _sample_layer_kernel(sinks_ref, x_ref, ck_ref, cv_ref, st_ref, cos_ref, sp_ref, sn_ref, vec_ref,
                         win_ref, seg_ref, cw_ref, wout_ref, wup_ref, wdn_ref,
                         y_ref, sk_ref, sv_ref, sc_ref,
                         q_scr, kx_scr, vx_scr, ux_scr, cat_scr, *, layer, n_seq, t_new):
    x = x_ref[...]
    h = _rms_norm(x, _vec(vec_ref, ROW_G1)).astype(BF16)
    z = _dot(h, win_ref[...])

    cos, sp, sn = cos_ref[...], sp_ref[...], sn_ref[...]
    seg = seg_ref[...]
    q = _head_norm_rope(z[:, :Q_COLS], _vec(vec_ref, ROW_QG, LANES), seg, cos, sp, sn) * Q_SCALE
    q_scr[...] = q.astype(BF16)
    k = _head_norm_rope(z[:, Q_COLS:Q_COLS + KV_COLS], _vec(vec_ref, ROW_KG, LANES), seg, cos, sp, sn)
    v = z[:, Q_COLS + KV_COLS:Q_COLS + 2 * KV_COLS]
    a_off = Q_COLS + 2 * KV_COLS
    u = z[:, a_off:a_off + CONV_CH] * jax.nn.sigmoid(z[:, a_off + CONV_CH:a_off + 2 * CONV_CH])

    kx_scr[:, 0:WINDOW, :] = ck_ref[...]
    vx_scr[:, 0:WINDOW, :] = cv_ref[...]
    kx_scr[:, WINDOW:WINDOW + t_new, :] = k.reshape(n_seq, t_new, LANES)
    vx_scr[:, WINDOW:WINDOW + t_new, :] = v.reshape(n_seq, t_new, LANES)
    ux_scr[:, CONV_SHIFT:CONV_HALO, :] = st_ref[...]
    ux_scr[:, CONV_HALO:CONV_HALO + t_new, :] = u.reshape(n_seq, t_new, CONV_CH)
    sk_ref[...] = kx_scr[:, t_new:t_new + WINDOW, :]
    sv_ref[...] = vx_scr[:, t_new:t_new + WINDOW, :]
    sc_ref[...] = ux_scr[:, CONV_HALO + t_new - CONV_STATE:CONV_HALO + t_new, :]

    beta = _vec(vec_ref, ROW_BETA)
    cb, ln_g, ln_b = (_vec(vec_ref, r, CONV_CH) for r in (ROW_CONV_B, ROW_LN_G, ROW_LN_B))
    sink_col = _sink_column(sinks_ref, layer, t_new)

    def per_group(i, carry):
        for j in range(SAMPLE_UNROLL):
            b = i * SAMPLE_UNROLL + j
            r0 = pl.multiple_of(b * t_new, t_new)
            a_o = _attention(q_scr[pl.ds(r0, t_new), :], kx_scr[b].astype(BF16), vx_scr[b].astype(BF16),
                             sink_col, None)
            cat_scr[pl.ds(r0, t_new), 0:ATTN_WIDTH] = (a_o * beta[:, 0:ATTN_WIDTH]).astype(BF16)
            window = lambda tap, b=b: ux_scr[b, CONV_SHIFT + tap:CONV_SHIFT + tap + t_new, :]
            c_o = _conv_ln_silu(window, cw_ref, cb, ln_g, ln_b)
            cat_scr[pl.ds(r0, t_new), ATTN_WIDTH:] = (c_o * beta[:, ATTN_WIDTH:]).astype(BF16)
        return carry

    lax.fori_loop(0, n_seq // SAMPLE_UNROLL, per_group, 0)

    x1 = x + _dot(cat_scr[...], wout_ref[...])
    h2 = _rms_norm(x1, _vec(vec_ref, ROW_G2)).astype(BF16)
    y = x1
    for c in range(SAMPLE_FFN_PIECES):
        y = y + _ffn_part(h2, wup_ref, wdn_ref, c * (D_FF // SAMPLE_FFN_PIECES), D_FF // SAMPLE_FFN_PIECES)
    y_ref[...] = y


def _layer_block(layer, *shape):
    return pl.BlockSpec((None,) + shape, lambda *_: (layer,) + (0,) * len(shape),
                        pipeline_mode=pl.Buffered(1))


def _smem():
    return pl.BlockSpec(memory_space=pltpu.SMEM)


def _layer_weight_specs(layer):
    return [
        _layer_block(layer, N_VEC_ROWS, D_MODEL),
        _layer_block(layer, D_MODEL, IN_COLS),
        pl.BlockSpec((LANES, LANES), lambda *_: (0, 0), pipeline_mode=pl.Buffered(1)),
        _layer_block(layer, CONV_WIDTH, SUBLANES, CONV_CH),
        _layer_block(layer, D_MODEL, D_MODEL),
        _layer_block(layer, D_MODEL, D_FF),
        _layer_block(layer, D_FF, D_MODEL),
    ]


def _prompt_layer(layer, x, sinks, tables, weights, tile):
    batch, seq, _ = x.shape
    n_s = seq // tile
    n_tiles = batch * n_s
    mixer_tile = lambda t: jnp.minimum(t, n_tiles - 1)
    ffn_tile = lambda t: jnp.maximum(t - 1, 0)
    table_spec = lambda j: pl.BlockSpec((tile, LANES), lambda t: (mixer_tile(t) % n_s, j))
    per_batch = lambda rows, cols: pl.BlockSpec((1, rows, cols), lambda t: (mixer_tile(t) // n_s, 0, 0))
    return pl.pallas_call(
        functools.partial(_prompt_layer_kernel, layer=layer, tile=tile, n_s=n_s, n_tiles=n_tiles),
        grid=(n_tiles + 1,),
        in_specs=[_smem(),
                  pl.BlockSpec((1, tile, D_MODEL), lambda t: (mixer_tile(t) // n_s, mixer_tile(t) % n_s, 0)),
                  table_spec(0), table_spec(1), table_spec(2)] + _layer_weight_specs(layer),
        out_specs=[pl.BlockSpec((1, tile, D_MODEL), lambda t: (ffn_tile(t) // n_s, ffn_tile(t) % n_s, 0)),
                   per_batch(WINDOW, KV_COLS), per_batch(WINDOW, KV_COLS),
                   per_batch(CONV_STATE, CONV_CH)],
        out_shape=[jax.ShapeDtypeStruct((batch, seq, D_MODEL), F32),
                   jax.ShapeDtypeStruct((batch, WINDOW, KV_COLS), F32),
                   jax.ShapeDtypeStruct((batch, WINDOW, KV_COLS), F32),
                   jax.ShapeDtypeStruct((batch, CONV_STATE, CONV_CH), F32)],
        scratch_shapes=[pltpu.VMEM((tile, Q_COLS), BF16),
                        pltpu.VMEM((WINDOW + tile, LANES), BF16),
                        pltpu.VMEM((WINDOW + tile, LANES), BF16),
                        pltpu.VMEM((SUBLANES, CONV_HALO + tile, CONV_CH), F32),
                        pltpu.VMEM((tile, D_MODEL), BF16),
                        pltpu.VMEM((tile, D_MODEL), F32),
                        pltpu.VMEM((2, tile, D_MODEL), BF16)],
        compiler_params=pltpu.CompilerParams(
            dimension_semantics=("arbitrary",),
            vmem_limit_bytes=VMEM_LIMIT_BYTES),
        name="prompt_layer",
    )(sinks, x, tables, tables, tables, *weights)


def _sample_layer(layer, x, cache_k, cache_v, state, sinks, tables, weights):
    n_seq, t_new, _ = x.shape
    n_tok = n_seq * t_new
    full = lambda *shape: pl.BlockSpec(shape, lambda i: (0,) * len(shape))
    once = lambda *shape: pl.BlockSpec(shape, lambda i: (0,) * len(shape), pipeline_mode=pl.Buffered(1))
    table = lambda j: pl.BlockSpec((n_tok, LANES), lambda i: (0, j), pipeline_mode=pl.Buffered(1))
    y, sk, sv, sc = pl.pallas_call(
        functools.partial(_sample_layer_kernel, layer=layer, n_seq=n_seq, t_new=t_new),
        grid=(1,),
        in_specs=[_smem(), once(n_tok, D_MODEL), _layer_block(layer, n_seq, WINDOW, KV_COLS),
                  _layer_block(layer, n_seq, WINDOW, KV_COLS), _layer_block(layer, n_seq, CONV_STATE, CONV_CH),
                  table(0), table(1), table(2)] + _layer_weight_specs(layer),
        out_specs=[full(n_tok, D_MODEL), full(n_seq, WINDOW, KV_COLS), full(n_seq, WINDOW, KV_COLS),
                   full(n_seq, CONV_STATE, CONV_CH)],
        out_shape=[jax.ShapeDtypeStruct((n_tok, D_MODEL), F32),
                   jax.ShapeDtypeStruct((n_seq, WINDOW, KV_COLS), F32),
                   jax.ShapeDtypeStruct((n_seq, WINDOW, KV_COLS), F32),
                   jax.ShapeDtypeStruct((n_seq, CONV_STATE, CONV_CH), F32)],
        scratch_shapes=[pltpu.VMEM((n_tok, Q_COLS), BF16),
                        pltpu.VMEM((n_seq, WINDOW + t_new, LANES), F32),
                        pltpu.VMEM((n_seq, WINDOW + t_new, LANES), F32),
                        pltpu.VMEM((n_seq, CONV_HALO + t_new, CONV_CH), F32),
                        pltpu.VMEM((n_tok, D_MODEL), BF16)],
        compiler_params=pltpu.CompilerParams(
            dimension_semantics=("arbitrary",),
            vmem_limit_bytes=VMEM_LIMIT_BYTES),
        name="sample_layer",
    )(sinks, x.reshape(n_tok, D_MODEL), cache_k, cache_v, state, tables, tables, tables, *weights)
    return y.reshape(n_seq, t_new, D_MODEL), sk, sv, sc


def _rope_tables(pos):
    half = ROT_DIM // 2
    inv = jnp.power(jnp.float32(ROPE_THETA), -jnp.arange(half, dtype=F32) * 2.0 / ROT_DIM)
    ang = pos.astype(F32)[:, None] * inv[None, :]
    base = jnp.concatenate([jnp.cos(ang), jnp.sin(ang)], axis=1)
    d = np.arange(LANES) % HEAD_DIM
    lane = np.arange(LANES)
    select = np.zeros((2 * half, 3 * LANES), np.float32)
    rot = d < ROT_DIM
    select[d[rot] % half, lane[rot]] = 1.0
    hi = (d >= half) & rot
    select[half + d[hi] - half, LANES + lane[hi]] = 1.0
    lo = d < half
    select[half + d[lo], 2 * LANES + lane[lo]] = -1.0
    ones = np.zeros((1, 3 * LANES), np.float32)
    ones[0, lane[~rot]] = 1.0
    return jnp.dot(base, jnp.asarray(select), precision=lax.Precision.HIGHEST) + jnp.asarray(ones)


def _slab_order(a, axis):
    shape = a.shape
    heads = a.reshape(shape[:axis] + (N_KV_HEADS, GROUP, HEAD_DIM) + shape[axis + 1:])
    return jnp.swapaxes(heads, axis, axis + 1).reshape(shape)


def _pack_weights(norm1_g, w_in, q_norm_g, k_norm_g, conv_w, conv_b, conv_ln_g, conv_ln_b,
                  beta_attn, beta_conv, w_out, norm2_g, w_up, w_down):
    depth = w_in.shape[0]
    wide = lambda a: jnp.pad(a.astype(F32), ((0, 0), (0, D_MODEL - a.shape[1])))
    rows = [None] * N_VEC_ROWS
    rows[ROW_G1], rows[ROW_G2] = norm1_g.astype(F32), norm2_g.astype(F32)
    rows[ROW_BETA] = jnp.concatenate([_slab_order(beta_attn, 1), beta_conv], axis=1).astype(F32)
    rows[ROW_CONV_B], rows[ROW_LN_G], rows[ROW_LN_B] = wide(conv_b), wide(conv_ln_g), wide(conv_ln_b)
    rows[ROW_QG] = wide(jnp.concatenate([q_norm_g, q_norm_g], axis=1))
    rows[ROW_KG] = wide(jnp.concatenate([k_norm_g, k_norm_g], axis=1))
    vec = jnp.stack(rows, axis=1)
    lane = jnp.arange(LANES) // HEAD_DIM
    seg_ones = (lane[:, None] == lane[None, :]).astype(BF16)
    w_in_s = jnp.concatenate([_slab_order(w_in[:, :, :Q_COLS], 2), w_in[:, :, Q_COLS:]], axis=2)
    w_out_s = jnp.concatenate([_slab_order(w_out[:, :ATTN_WIDTH], 1), w_out[:, ATTN_WIDTH:]], axis=1)
    assert vec.shape == (depth, N_VEC_ROWS, D_MODEL)
    conv_w8 = jnp.broadcast_to(conv_w.astype(F32)[:, :, None, :], (depth, CONV_WIDTH, SUBLANES, CONV_CH))
    return [vec, w_in_s.astype(BF16), seg_ones, conv_w8, w_out_s.astype(BF16),
            w_up.astype(BF16), w_down.astype(BF16)]


def kernel(x_prompt, x_sample, cache_k, cache_v, state_conv, norm1_g, w_in, q_norm_g, k_norm_g,
           attn_sinks, conv_w, conv_b, conv_ln_g, conv_ln_b, beta_attn, beta_conv, w_out,
           norm2_g, w_up, w_down):
    batch, seq, _ = x_prompt.shape
    n_seq, t_new, _ = x_sample.shape
    tile = min(PROMPT_TILE, seq)
    assert seq % tile == 0 and tile % CHUNK == 0 and tile >= WINDOW and D_FF % (tile // CHUNK) == 0
    assert n_seq % SAMPLE_UNROLL == 0
    tables_p = _rope_tables(jnp.arange(seq))
    tables_s = jnp.tile(_rope_tables(PAST_LEN + jnp.arange(t_new)), (n_seq, 1))
    weights = _pack_weights(norm1_g, w_in, q_norm_g, k_norm_g, conv_w, conv_b, conv_ln_g, conv_ln_b,
                            beta_attn, beta_conv, w_out, norm2_g, w_up, w_down)
    sinks = attn_sinks.astype(F32)
    cache_k = cache_k.reshape(DEPTH, n_seq, WINDOW, KV_COLS)
    cache_v = cache_v.reshape(DEPTH, n_seq, WINDOW, KV_COLS)
    yp, ys = x_prompt, x_sample
    outs = [[] for _ in range(6)]
    for l in range(DEPTH):
        yp, pk, pv, pc = _prompt_layer(l, yp, sinks, tables_p, weights, tile)
        ys, sk, sv, sc = _sample_layer(l, ys, cache_k, cache_v, state_conv, sinks, tables_s, weights)
        kv_shape = lambda a: a.reshape(a.shape[0], WINDOW, N_KV_HEADS, HEAD_DIM)
        for dst, val in zip(outs, (kv_shape(pk), kv_shape(pv), pc, kv_shape(sk), kv_shape(sv), sc)):
            dst.append(val)
    return (yp, ys) + tuple(jnp.stack(o) for o in outs)
```

```python
import functools

import jax
import jax.numpy as jnp
from jax import lax
from jax.experimental import pallas as pl
from jax.experimental.pallas import tpu as pltpu

D_MODEL = 1024
DEPTH = 2
PAST_LEN = 2048
CHUNK = 64
WINDOW = 128
HEAD_DIM = 64
ATTN_WIDTH = D_MODEL // 2
N_HEADS = ATTN_WIDTH // HEAD_DIM
N_KV_HEADS = max(1, N_HEADS // 4)
GROUP = N_HEADS // N_KV_HEADS
CONV_CH = D_MODEL - ATTN_WIDTH
CONV_WIDTH = 31
CONV_STATE = CONV_WIDTH - 1
ROT_DIM = HEAD_DIM // 4
ROPE_THETA = 500000.0
D_FF = 4 * D_MODEL
EPS = 1e-6
Q_COLS = N_HEADS * HEAD_DIM
KV_COLS = N_KV_HEADS * HEAD_DIM
IN_COLS = Q_COLS + 2 * KV_COLS + 2 * CONV_CH

LANES = 128
SUBLANES = 8
N_SLABS = Q_COLS // LANES
CONV_HALO = 32
CONV_SHIFT = CONV_HALO - CONV_STATE
PROMPT_TILE = 512
CONV_ROWS = 32
SAMPLE_UNROLL = 8
SAMPLE_FFN_PIECES = 4
VMEM_LIMIT_BYTES = 56 * 1024 * 1024

ROW_CONV_B, ROW_LN_G, ROW_LN_B, ROW_QG, ROW_KG = range(5)
N_VEC_ROWS = 5

LOG2_E = 1.4426950408889634
Q_SCALE = HEAD_DIM ** -0.5 * LOG2_E

F32 = jnp.float32
BF16 = jnp.bfloat16

assert N_KV_HEADS == 2 and KV_COLS == LANES and GROUP == N_SLABS


def _dot(a, b):
    return jnp.dot(a, b, preferred_element_type=F32)


def _dot_nt(a, b):
    return lax.dot_general(a, b, (((1,), (1,)), ((), ())), preferred_element_type=F32)


def _rms_unit(x):
    ms = jnp.mean(x * x, axis=-1, keepdims=True)
    return x * lax.rsqrt(ms + EPS)


def _lane_lo(shape):
    return lax.broadcasted_iota(jnp.int32, shape, len(shape) - 1) % LANES < HEAD_DIM


def _head_norm_rope(z, gain, seg_ones, cos, sin_prev, sin_next):
    out = []
    for i in range(z.shape[1] // LANES):
        zi = z[:, i * LANES:(i + 1) * LANES]
        ssq = _dot((zi * zi).astype(BF16), seg_ones)
        zn = zi * lax.rsqrt(ssq * (1.0 / HEAD_DIM) + EPS) * gain
        prev = pltpu.roll(zn, ROT_DIM // 2, axis=1)
        nxt = pltpu.roll(zn, LANES - ROT_DIM // 2, axis=1)
        out.append(zn * cos + prev * sin_prev + nxt * sin_next)
    return out[0] if len(out) == 1 else jnp.concatenate(out, axis=1)


def _sink_column(sinks_ref, layer, rows):
    n_blocks = 2 * N_SLABS
    blk = lax.broadcasted_iota(jnp.int32, (n_blocks * rows, 1), 0) // rows
    col = jnp.zeros((n_blocks * rows, 1), F32)
    for i in range(N_SLABS):
        for half in range(2):
            col = jnp.where(blk == 2 * i + half, sinks_ref[layer, i + GROUP * half] * LOG2_E, col)
    return col


def _attention(q, k, v, sink_col, valid):
    rows = q.shape[0]
    lo = _lane_lo((rows, LANES))
    zero = jnp.zeros((rows, LANES), q.dtype)
    parts = []
    for i in range(N_SLABS):
        slab = q[:, i * LANES:(i + 1) * LANES]
        parts += [jnp.where(lo, slab, zero), jnp.where(lo, zero, slab)]
    s = _dot_nt(jnp.concatenate(parts, axis=0), k)
    if valid is not None:
        s = jnp.where(valid, s, -jnp.inf)
    m = jnp.maximum(jnp.max(s, axis=-1, keepdims=True), sink_col)
    p = jnp.exp2(s - m)
    denom = jnp.sum(p, axis=-1, keepdims=True) + jnp.exp2(sink_col - m)
    o = _dot(p.astype(BF16), v) * (1.0 / denom)
    return jnp.concatenate(
        [jnp.where(lo, o[2 * i * rows:(2 * i + 1) * rows], o[(2 * i + 1) * rows:(2 * i + 2) * rows])
         for i in range(N_SLABS)], axis=1)


def _conv_ln_silu(u_window, cw_ref, cb, ln_g, ln_b):
    def tap(j):
        u = u_window(j)
        return u.reshape(u.shape[0] // SUBLANES, SUBLANES, u.shape[1]) * cw_ref[j][None]
    acc = tap(0)
    for j in range(1, CONV_WIDTH):
        acc = acc + tap(j)
    y = acc.reshape(acc.shape[0] * SUBLANES, acc.shape[2]) + cb
    mu = jnp.mean(y, axis=-1, keepdims=True)
    yc = y - mu
    var = jnp.mean(yc * yc, axis=-1, keepdims=True)
    yn = yc * lax.rsqrt(var + EPS) * ln_g + ln_b
    return yn * jax.nn.sigmoid(yn)


def _ffn_part(h2, wup_ref, wdn_ref, c0, width):
    up = jnp.maximum(_dot(h2, wup_ref[:, c0:c0 + width]).astype(BF16), 0.0)
    return _dot(up * up, wdn_ref[c0:c0 + width, :])


def _vec(vec_ref, row, width=D_MODEL):
    return vec_ref[row:row + 1, 0:width]


def _prompt_layer_kernel(sinks_ref, x_ref, cos_ref, sp_ref, sn_ref, vec_ref, win_ref, seg_ref, cw_ref,
                         wout_ref, wup_ref, wdn_ref,
                         y_ref, pk_ref, pv_ref, pc_ref,
                         q_scr, k_scr, v_scr, u_scr, cat_scr, x1_scr, h2_scr,
                         *, layer, tile, n_s, n_tiles):
    t = pl.program_id(0)
    s_idx = jnp.minimum(t, n_tiles - 1) % n_s
    slot = t % 2
    n_pieces = tile // CHUNK
    ffn_width = D_FF // n_pieces

    @pl.when(t == 0)
    def _():
        x1_scr[...] = jnp.zeros(x1_scr.shape, F32)
        h2_scr[...] = jnp.zeros(h2_scr.shape, BF16)

    @pl.when(s_idx == 0)
    def _():
        k_scr[0:WINDOW, :] = jnp.zeros((WINDOW, LANES), BF16)
        v_scr[0:WINDOW, :] = jnp.zeros((WINDOW, LANES), BF16)
        u_scr[0, 0:CONV_HALO, :] = jnp.zeros((CONV_HALO, CONV_CH), F32)

    def ffn_piece(c):
        part = _ffn_part(h2_scr[slot], wup_ref, wdn_ref, c * ffn_width, ffn_width)
        if c == 0:
            y_ref[0] = x1_scr[...] + part
        else:
            y_ref[0] += part

    ffn_piece(0)

    h = _rms_unit(x_ref[0]).astype(BF16)
    z = _dot(h, win_ref[...])
    cos, sp, sn = cos_ref[...], sp_ref[...], sn_ref[...]
    seg = seg_ref[...]
    q = _head_norm_rope(z[:, :Q_COLS], _vec(vec_ref, ROW_QG, LANES), seg, cos, sp, sn) * Q_SCALE
    q_scr[...] = q.astype(BF16)
    k = _head_norm_rope(z[:, Q_COLS:Q_COLS + KV_COLS], _vec(vec_ref, ROW_KG, LANES), seg, cos, sp, sn)
    v = z[:, Q_COLS + KV_COLS:Q_COLS + 2 * KV_COLS]
    k_scr[WINDOW:WINDOW + tile, :] = k.astype(BF16)
    v_scr[WINDOW:WINDOW + tile, :] = v.astype(BF16)
    pk_ref[0] = k[tile - WINDOW:tile, :]
    pv_ref[0] = v[tile - WINDOW:tile, :]
    a_off = Q_COLS + 2 * KV_COLS
    u = z[:, a_off:a_off + CONV_CH] * jax.nn.sigmoid(z[:, a_off + CONV_CH:a_off + 2 * CONV_CH])
    u_scr[0, CONV_HALO:CONV_HALO + tile, :] = u
    pc_ref[0] = u_scr[0, CONV_HALO + tile - CONV_STATE:CONV_HALO + tile, :]
    n_shift_rows = CONV_HALO + tile - SUBLANES
    n_groups = (CONV_HALO + tile) // SUBLANES
    u_groups = u_scr[0].reshape(n_groups, SUBLANES, CONV_CH)
    row_in_group = lax.broadcasted_iota(jnp.int32, (n_groups - 1, SUBLANES, CONV_CH), 1)
    for b in range(1, SUBLANES):
        rot = pltpu.roll(u_groups, SUBLANES - b, axis=1)
        shifted = jnp.where(row_in_group < SUBLANES - b, rot[:-1], rot[1:])
        u_scr[b, 0:n_shift_rows, :] = shifted.reshape(n_shift_rows, CONV_CH)

    cb, ln_g, ln_b = (_vec(vec_ref, r, CONV_CH) for r in (ROW_CONV_B, ROW_LN_G, ROW_LN_B))
    sink_col = _sink_column(sinks_ref, layer, CHUNK)
    n_keys = WINDOW + CHUNK
    key_j = lax.broadcasted_iota(jnp.int32, (1, n_keys), 1)

    def mixer_piece(c):
        r0 = c * CHUNK
        valid = None
        if r0 < WINDOW:
            valid = jnp.logical_or(key_j >= WINDOW - r0, s_idx > 0)
        a_o = _attention(q_scr[r0:r0 + CHUNK, :], k_scr[r0:r0 + n_keys, :], v_scr[r0:r0 + n_keys, :],
                         sink_col, valid)
        cat_scr[r0:r0 + CHUNK, 0:ATTN_WIDTH] = a_o.astype(BF16)

        for r in range(r0, r0 + CHUNK, CONV_ROWS):
            def window(j, r=r):
                off = CONV_SHIFT + j
                b = off % SUBLANES
                return u_scr[b, r + off - b:r + off - b + CONV_ROWS, :]
            c_o = _conv_ln_silu(window, cw_ref, cb, ln_g, ln_b)
            cat_scr[r:r + CONV_ROWS, ATTN_WIDTH:] = c_o.astype(BF16)

    for c in range(n_pieces):
        if c + 1 < n_pieces - 1:
            ffn_piece(c + 1)
        mixer_piece(c)

    k_scr[0:WINDOW, :] = k_scr[tile:tile + WINDOW, :]
    v_scr[0:WINDOW, :] = v_scr[tile:tile + WINDOW, :]
    u_scr[0, 0:CONV_HALO, :] = u_scr[0, tile:tile + CONV_HALO, :]
    x1 = x_ref[0] + _dot(cat_scr[...], wout_ref[...])
    ffn_piece(n_pieces - 1)
    x1_scr[...] = x1
    h2_scr[1 - slot] = _rms_unit(x1).astype(BF16)


def _sample_layer_kernel(sinks_ref, x_ref, ck_ref, cv_ref, st_ref, cos_ref, sp_ref, sn_ref, vec_ref,
                         win_ref, seg_ref, cw_ref, wout_ref, wup_ref, wdn_ref,
                         y_ref, sk_ref, sv_ref, sc_ref,
                         q_scr, kx_scr, vx_scr, ux_scr, cat_scr, *, layer, n_seq, t_new):
    x = x_ref[...]
    h = _rms_unit(x).astype(BF16)
    z = _dot(h, win_ref[...])

    cos, sp, sn = cos_ref[...], sp_ref[...], sn_ref[...]
    seg = seg_ref[...]
    q = _head_norm_rope(z[:, :Q_COLS], _vec(vec_ref, ROW_QG, LANES), seg, cos, sp, sn) * Q_SCALE
    q_scr[...] = q.astype(BF16)
    k = _head_norm_rope(z[:, Q_COLS:Q_COLS + KV_COLS], _vec(vec_ref, ROW_KG, LANES), seg, cos, sp, sn)
    v = z[:, Q_COLS + KV_COLS:Q_COLS + 2 * KV_COLS]
    a_off = Q_COLS + 2 * KV_COLS
    u = z[:, a_off:a_off + CONV_CH] * jax.nn.sigmoid(z[:, a_off + CONV_CH:a_off + 2 * CONV_CH])

    kx_scr[:, 0:WINDOW, :] = ck_ref[...]
    vx_scr[:, 0:WINDOW, :] = cv_ref[...]
    kx_scr[:, WINDOW:WINDOW + t_new, :] = k.reshape(n_seq, t_new, LANES)
    vx_scr[:, WINDOW:WINDOW + t_new, :] = v.reshape(n_seq, t_new, LANES)
    ux_scr[:, CONV_SHIFT:CONV_HALO, :] = st_ref[...]
    ux_scr[:, CONV_HALO:CONV_HALO + t_new, :] = u.reshape(n_seq, t_new, CONV_CH)
    sk_ref[...] = kx_scr[:, t_new:t_new + WINDOW, :]
    sv_ref[...] = vx_scr[:, t_new:t_new + WINDOW, :]
    sc_ref[...] = ux_scr[:, CONV_HALO + t_new - CONV_STATE:CONV_HALO + t_new, :]

    cb, ln_g, ln_b = (_vec(vec_ref, r, CONV_CH) for r in (ROW_CONV_B, ROW_LN_G, ROW_LN_B))
    sink_col = _sink_column(sinks_ref, layer, t_new)

    def per_group(i, carry):
        for j in range(SAMPLE_UNROLL):
            b = i * SAMPLE_UNROLL + j
            r0 = pl.multiple_of(b * t_new, t_new)
            a_o = _attention(q_scr[pl.ds(r0, t_new), :], kx_scr[b].astype(BF16), vx_scr[b].astype(BF16),
                             sink_col, None)
            cat_scr[pl.ds(r0, t_new), 0:ATTN_WIDTH] = a_o.astype(BF16)
            window = lambda tap, b=b: ux_scr[b, CONV_SHIFT + tap:CONV_SHIFT + tap + t_new, :]
            c_o = _conv_ln_silu(window, cw_ref, cb, ln_g, ln_b)
            cat_scr[pl.ds(r0, t_new), ATTN_WIDTH:] = c_o.astype(BF16)
        return carry

    lax.fori_loop(0, n_seq // SAMPLE_UNROLL, per_group, 0)

    x1 = x + _dot(cat_scr[...], wout_ref[...])
    h2 = _rms_unit(x1).astype(BF16)
    y = x1
    for c in range(SAMPLE_FFN_PIECES):
        y = y + _ffn_part(h2, wup_ref, wdn_ref, c * (D_FF // SAMPLE_FFN_PIECES), D_FF // SAMPLE_FFN_PIECES)
    y_ref[...] = y


def _layer_block(layer, *shape):
    return pl.BlockSpec((None,) + shape, lambda *_: (layer,) + (0,) * len(shape),
                        pipeline_mode=pl.Buffered(1))


def _smem():
    return pl.BlockSpec(memory_space=pltpu.SMEM)


def _layer_weight_specs(layer):
    return [
        _layer_block(layer, N_VEC_ROWS, D_MODEL),
        _layer_block(layer, D_MODEL, IN_COLS),
        pl.BlockSpec((LANES, LANES), lambda *_: (0, 0), pipeline_mode=pl.Buffered(1)),
        _layer_block(layer, CONV_WIDTH, SUBLANES, CONV_CH),
        _layer_block(layer, D_MODEL, D_MODEL),
        _layer_block(layer, D_MODEL, D_FF),
        _layer_block(layer, D_FF, D_MODEL),
    ]


def _prompt_layer(layer, x, sinks, tables, weights, tile):
    batch, seq, _ = x.shape
    n_s = seq // tile
    n_tiles = batch * n_s
    mixer_tile = lambda t: jnp.minimum(t, n_tiles - 1)
    ffn_tile = lambda t: jnp.maximum(t - 1, 0)
    table_spec = pl.BlockSpec((tile, LANES), lambda t: (mixer_tile(t) % n_s, 0))
    per_batch = lambda rows, cols: pl.BlockSpec((1, rows, cols), lambda t: (mixer_tile(t) // n_s, 0, 0))
    return pl.pallas_call(
        functools.partial(_prompt_layer_kernel, layer=layer, tile=tile, n_s=n_s, n_tiles=n_tiles),
        grid=(n_tiles + 1,),
        in_specs=[_smem(),
                  pl.BlockSpec((1, tile, D_MODEL), lambda t: (mixer_tile(t) // n_s, mixer_tile(t) % n_s, 0)),
                  table_spec, table_spec, table_spec] + _layer_weight_specs(layer),
        out_specs=[pl.BlockSpec((1, tile, D_MODEL), lambda t: (ffn_tile(t) // n_s, ffn_tile(t) % n_s, 0)),
                   per_batch(WINDOW, KV_COLS), per_batch(WINDOW, KV_COLS),
                   per_batch(CONV_STATE, CONV_CH)],
        out_shape=[jax.ShapeDtypeStruct((batch, seq, D_MODEL), F32),
                   jax.ShapeDtypeStruct((batch, WINDOW, KV_COLS), F32),
                   jax.ShapeDtypeStruct((batch, WINDOW, KV_COLS), F32),
                   jax.ShapeDtypeStruct((batch, CONV_STATE, CONV_CH), F32)],
        scratch_shapes=[pltpu.VMEM((tile, Q_COLS), BF16),
                        pltpu.VMEM((WINDOW + tile, LANES), BF16),
                        pltpu.VMEM((WINDOW + tile, LANES), BF16),
                        pltpu.VMEM((SUBLANES, CONV_HALO + tile, CONV_CH), F32),
                        pltpu.VMEM((tile, D_MODEL), BF16),
                        pltpu.VMEM((tile, D_MODEL), F32),
                        pltpu.VMEM((2, tile, D_MODEL), BF16)],
        compiler_params=pltpu.CompilerParams(
            dimension_semantics=("arbitrary",),
            vmem_limit_bytes=VMEM_LIMIT_BYTES),
        name="prompt_layer",
    )(sinks, x, *tables, *weights)


def _sample_layer(layer, x, cache_k, cache_v, state, sinks, tables, weights):
    n_seq, t_new, _ = x.shape
    n_tok = n_seq * t_new
    full = lambda *shape: pl.BlockSpec(shape, lambda i: (0,) * len(shape))
    once = lambda *shape: pl.BlockSpec(shape, lambda i: (0,) * len(shape), pipeline_mode=pl.Buffered(1))
    y, sk, sv, sc = pl.pallas_call(
        functools.partial(_sample_layer_kernel, layer=layer, n_seq=n_seq, t_new=t_new),
        grid=(1,),
        in_specs=[_smem(), once(n_tok, D_MODEL), _layer_block(layer, n_seq, WINDOW, KV_COLS),
                  _layer_block(layer, n_seq, WINDOW, KV_COLS), _layer_block(layer, n_seq, CONV_STATE, CONV_CH),
                  once(n_tok, LANES), once(n_tok, LANES), once(n_tok, LANES)] + _layer_weight_specs(layer),
        out_specs=[full(n_tok, D_MODEL), full(n_seq, WINDOW, KV_COLS), full(n_seq, WINDOW, KV_COLS),
                   full(n_seq, CONV_STATE, CONV_CH)],
        out_shape=[jax.ShapeDtypeStruct((n_tok, D_MODEL), F32),
                   jax.ShapeDtypeStruct((n_seq, WINDOW, KV_COLS), F32),
                   jax.ShapeDtypeStruct((n_seq, WINDOW, KV_COLS), F32),
                   jax.ShapeDtypeStruct((n_seq, CONV_STATE, CONV_CH), F32)],
        scratch_shapes=[pltpu.VMEM((n_tok, Q_COLS), BF16),
                        pltpu.VMEM((n_seq, WINDOW + t_new, LANES), F32),
                        pltpu.VMEM((n_seq, WINDOW + t_new, LANES), F32),
                        pltpu.VMEM((n_seq, CONV_HALO + t_new, CONV_CH), F32),
                        pltpu.VMEM((n_tok, D_MODEL), BF16)],
        compiler_params=pltpu.CompilerParams(
            dimension_semantics=("arbitrary",),
            vmem_limit_bytes=VMEM_LIMIT_BYTES),
        name="sample_layer",
    )(sinks, x.reshape(n_tok, D_MODEL), cache_k, cache_v, state, *tables, *weights)
    return y.reshape(n_seq, t_new, D_MODEL), sk, sv, sc


def _rope_tables(pos):
    half = ROT_DIM // 2
    d = jnp.arange(LANES) % HEAD_DIM
    inv = jnp.power(jnp.float32(ROPE_THETA), -(d % half).astype(F32) * 2.0 / ROT_DIM)
    ang = pos.astype(F32)[:, None] * inv[None, :]
    cos, sin = jnp.cos(ang), jnp.sin(ang)
    d = d[None, :]
    return (jnp.where(d < ROT_DIM, cos, 1.0),
            jnp.where((d >= half) & (d < ROT_DIM), sin, 0.0),
            jnp.where(d < half, -sin, 0.0))


def _slab_order(a, axis):
    shape = a.shape
    heads = a.reshape(shape[:axis] + (N_KV_HEADS, GROUP, HEAD_DIM) + shape[axis + 1:])
    return jnp.swapaxes(heads, axis, axis + 1).reshape(shape)


def _pack_weights(norm1_g, w_in, q_norm_g, k_norm_g, conv_w, conv_b, conv_ln_g, conv_ln_b,
                  beta_attn, beta_conv, w_out, norm2_g, w_up, w_down):
    depth = w_in.shape[0]
    wide = lambda a: jnp.pad(a.astype(F32), ((0, 0), (0, D_MODEL - a.shape[1])))
    rows = [None] * N_VEC_ROWS
    rows[ROW_CONV_B], rows[ROW_LN_G], rows[ROW_LN_B] = wide(conv_b), wide(conv_ln_g), wide(conv_ln_b)
    rows[ROW_QG] = wide(jnp.concatenate([q_norm_g, q_norm_g], axis=1))
    rows[ROW_KG] = wide(jnp.concatenate([k_norm_g, k_norm_g], axis=1))
    vec = jnp.stack(rows, axis=1)
    lane = jnp.arange(LANES) // HEAD_DIM
    seg_ones = (lane[:, None] == lane[None, :]).astype(BF16)
    w_in_s = jnp.concatenate([_slab_order(w_in[:, :, :Q_COLS], 2), w_in[:, :, Q_COLS:]], axis=2) * norm1_g[:, :, None]
    beta = jnp.concatenate([beta_attn, beta_conv], axis=1)
    w_out_g = w_out * beta[:, :, None]
    w_out_s = jnp.concatenate([_slab_order(w_out_g[:, :ATTN_WIDTH], 1), w_out_g[:, ATTN_WIDTH:]], axis=1)
    w_up = w_up * norm2_g[:, :, None]
    assert vec.shape == (depth, N_VEC_ROWS, D_MODEL)
    conv_w8 = jnp.broadcast_to(conv_w.astype(F32)[:, :, None, :], (depth, CONV_WIDTH, SUBLANES, CONV_CH))
    return [vec, w_in_s.astype(BF16), seg_ones, conv_w8, w_out_s.astype(BF16),
            w_up.astype(BF16), w_down.astype(BF16)]


def kernel(x_prompt, x_sample, cache_k, cache_v, state_conv, norm1_g, w_in, q_norm_g, k_norm_g,
           attn_sinks, conv_w, conv_b, conv_ln_g, conv_ln_b, beta_attn, beta_conv, w_out,
           norm2_g, w_up, w_down):
    batch, seq, _ = x_prompt.shape
    n_seq, t_new, _ = x_sample.shape
    tile = min(PROMPT_TILE, seq)
    assert seq % tile == 0 and tile % CHUNK == 0 and tile >= WINDOW and D_FF % (tile // CHUNK) == 0
    assert n_seq % SAMPLE_UNROLL == 0
    tables_p = _rope_tables(jnp.arange(seq))
    tables_s = tuple(jnp.tile(t, (n_seq, 1)) for t in _rope_tables(PAST_LEN + jnp.arange(t_new)))
    weights = _pack_weights(norm1_g, w_in, q_norm_g, k_norm_g, conv_w, conv_b, conv_ln_g, conv_ln_b,
                            beta_attn, beta_conv, w_out, norm2_g, w_up, w_down)
    sinks = attn_sinks.astype(F32)
    cache_k = cache_k.reshape(DEPTH, n_seq, WINDOW, KV_COLS)
    cache_v = cache_v.reshape(DEPTH, n_seq, WINDOW, KV_COLS)
    yp, ys = x_prompt, x_sample
    outs = [[] for _ in range(6)]
    for l in range(DEPTH):
        yp, pk, pv, pc = _prompt_layer(l, yp, sinks, tables_p, weights, tile)
        ys, sk, sv, sc = _sample_layer(l, ys, cache_k, cache_v, state_conv, sinks, tables_s, weights)
        kv_shape = lambda a: a.reshape(a.shape[0], WINDOW, N_KV_HEADS, HEAD_DIM)
        for dst, val in zip(outs, (kv_shape(pk), kv_shape(pv), pc, kv_shape(sk), kv_shape(sv), sc)):
            dst.append(val)
    return (yp, ys) + tuple(jnp.stack(o) for o in outs)
```
